```python
import jax, jax.numpy as jnp
from jax import lax
import numpy as np

D_MODEL = 1024
BATCH = 2
SEQ = 8192
DEPTH = 2
DEC_BATCH = 32
DEC_SEQ = 4
PAST_LEN = 8192
PAGE_SIZE = 128

N_EVEN = (DEPTH + 1) // 2
N_ODD = DEPTH // 2
ML_HEADS = 4
ML_DK = 128
ML_DV = 128
ML_WIDTH = ML_HEADS * ML_DV
ML_CHUNK = 128
SWA_GROUPS = ((128, 1), (512, 4), (2048, 16))
SWA_HEADS = 4
SWA_HD = 64
SWA_WIDTH = len(SWA_GROUPS) * SWA_HEADS * SWA_HD
ROT_DIM = SWA_HD // 4
ROPE_THETA = 500000.0
HG_HEADS = 8
HG_DK = 128
HG_DV = D_MODEL // HG_HEADS
HG_CHUNK = 64
PEER_KEYS = 128
PEER_EXPERTS = PEER_KEYS * PEER_KEYS
PEER_HEADS = 8
PEER_DKEY = 128
PEER_TOPK = 16
PEER_BLOCK = 256
EPS = 1e-6

EVEN_IN = 4 * ML_WIDTH + 2 * ML_HEADS + 3 * SWA_WIDTH
EVEN_OUT = ML_WIDTH + SWA_HEADS * SWA_HD
ODD_IN = 2 * HG_HEADS * HG_DK + 2 * HG_HEADS * HG_DV

kernel_name = 'hybrid_mlstm_dilswa_hgrn2_peer_step'


def rms_norm(x, g):
    xf = x.astype(jnp.float32)
    y = xf * lax.rsqrt(jnp.mean(xf * xf, axis=-1, keepdims=True) + EPS)
    return (y * g).astype(x.dtype)


def ada_params(c, w, b):
    mod = (jax.nn.silu(c) @ w + b)[:, None, :]
    return jnp.split(mod, 6, axis=-1)


def rotary(x, pos):
    half = ROT_DIM // 2
    inv = ROPE_THETA ** (-jnp.arange(half, dtype=jnp.float32) / half)
    ang = pos.astype(jnp.float32)[:, None] * inv
    shp = (1, ang.shape[0]) + (1,) * (x.ndim - 3) + (half,)
    cos, sin = jnp.cos(ang).reshape(shp), jnp.sin(ang).reshape(shp)
    xr = x[..., :ROT_DIM].astype(jnp.float32)
    x1, x2 = xr[..., :half], xr[..., half:]
    rot = jnp.concatenate([x1 * cos - x2 * sin, x1 * sin + x2 * cos], axis=-1)
    return jnp.concatenate([rot.astype(x.dtype), x[..., ROT_DIM:]], axis=-1)


def mlstm_chunk(carry, inp):
    C, n, m = carry
    q, k, v, ig, lf = inp
    L = q.shape[1]
    Fh = jnp.cumsum(lf, axis=1).transpose(0, 2, 1)
    igh = ig.transpose(0, 2, 1)
    causal = jnp.tril(jnp.ones((L, L), bool))
    logD = jnp.where(causal, Fh[..., :, None] - Fh[..., None, :] + igh[..., None, :], -jnp.inf)
    inter = Fh + m[..., None]
    m_loc = jnp.maximum(inter, jnp.max(logD, axis=-1))
    w_intra = jnp.exp(logD - m_loc[..., None])
    w_inter = jnp.exp(inter - m_loc)
    s = jnp.einsum('blhd,bshd->bhls', q, k) * w_intra
    num = jnp.einsum('bhls,bshv->blhv', s, v) + w_inter.transpose(0, 2, 1)[..., None] * jnp.einsum('blhd,bhdv->blhv', q, C)
    den = jnp.sum(s, axis=-1) + w_inter * jnp.einsum('blhd,bhd->bhl', q, n)
    den = jnp.maximum(jnp.abs(den), jnp.exp(-m_loc)).transpose(0, 2, 1)
    h = num / den[..., None]
    FL = Fh[..., -1]
    logw = FL[..., None] - Fh + igh
    m_new = jnp.maximum(FL + m, jnp.max(logw, axis=-1))
    wk = jnp.exp(logw - m_new[..., None])
    decay = jnp.exp(FL + m - m_new)
    C_new = decay[..., None, None] * C + jnp.einsum('bhs,bshd,bshv->bhdv', wk, k, v)
    n_new = decay[..., None] * n + jnp.einsum('bhs,bshd->bhd', wk, k)
    return (C_new, n_new, m_new), h


def mlstm(q, k, v, ig, lf, state):
    B, T = q.shape[:2]
    L = ML_CHUNK if T % ML_CHUNK == 0 else T
    nc = T // L
    chunks = lambda a: jnp.moveaxis(a.reshape((B, nc, L) + a.shape[2:]), 1, 0)
    init = tuple(s.astype(jnp.float32) for s in state)
    carry, h = lax.scan(mlstm_chunk, init, tuple(chunks(a) for a in (q, k, v, ig, lf)))
    return jnp.moveaxis(h, 0, 1).reshape(B, T, ML_HEADS, ML_DV), carry


def dilated_prompt(q, k, v, window, dil):
    f32 = jnp.float32
    B, T, H, HD = q.shape
    n = window // dil
    Tp = -(-T // window) * window
    nb = Tp // window

    def split(a):
        a = jnp.pad(a.astype(f32), ((0, 0), (0, Tp - T), (0, 0), (0, 0))).reshape(B, Tp // dil, dil, H, HD)
        return jnp.moveaxis(a, 2, 1).reshape(B, dil, nb, n, H, HD)

    qs, ks, vs = split(q), split(k), split(v)
    prev = lambda a: jnp.pad(a, ((0, 0), (0, 0), (1, 0), (0, 0), (0, 0), (0, 0)))[:, :, :-1]
    kk = jnp.concatenate([prev(ks), ks], axis=3)
    vv = jnp.concatenate([prev(vs), vs], axis=3)
    a_off = jnp.arange(n)[:, None]
    k_off = jnp.arange(2 * n)[None, :]
    band = (k_off >= a_off) & (k_off <= a_off + n)
    first = (jnp.arange(nb)[:, None, None] > 0) | (k_off[None] >= n)
    mask = band[None] & first
    s = jnp.einsum('brcqhd,brckhd->brchqk', qs, kk) * SWA_HD ** -0.5
    s = jnp.where(mask[None, None, :, None], s, -jnp.inf)
    mx = jnp.max(s, axis=-1, keepdims=True)
    p = jnp.exp(s - mx)
    den = jnp.sum(p, axis=-1)
    o = jnp.einsum('brchqk,brckhd->brcqhd', p, vv) / jnp.swapaxes(den, -1, -2)[..., None]
    lse = jnp.swapaxes(mx[..., 0] + jnp.log(den), -1, -2)
    unsplit = lambda a: jnp.moveaxis(a.reshape((B, dil, Tp // dil) + a.shape[4:]), 1, 2).reshape((B, Tp) + a.shape[4:])[:, :T]
    return unsplit(o), unsplit(lse)


def dilated_sample(q, k, v, buf, window, dil):
    f32 = jnp.float32
    L, LW = q.shape[1], buf.shape[1]
    n = window // dil
    kk = jnp.concatenate([buf[:, :, 0].astype(f32), k.astype(f32)], axis=1)
    vv = jnp.concatenate([buf[:, :, 1].astype(f32), v.astype(f32)], axis=1)
    idx = LW + jnp.arange(L)[:, None] - dil * jnp.arange(n + 1)[None, :]
    valid = idx >= 0
    idx = jnp.maximum(idx, 0)
    kg, vg = kk[:, idx], vv[:, idx]
    s = jnp.einsum('blhd,bljhd->blhj', q.astype(f32), kg) * SWA_HD ** -0.5
    s = jnp.where(valid[None, :, None, :], s, -jnp.inf)
    mx = jnp.max(s, axis=-1, keepdims=True)
    p = jnp.exp(s - mx)
    den = jnp.sum(p, axis=-1)
    o = jnp.einsum('blhj,bljhd->blhd', p, vg) / den[..., None]
    return o, mx[..., 0] + jnp.log(den)


def even_mixer(h, pos, w_in, b_if, ml_gain, qn_g, kn_g, w_out, ml_state, bufs):
    f32 = jnp.float32
    B, T, _ = h.shape
    sizes = [ML_WIDTH] * 4 + [2 * ML_HEADS] + [SWA_WIDTH] * 2
    cuts = [int(c) for c in np.cumsum(sizes)]
    mq, mk, mv, mo, mg, aq, ak, av = jnp.split(h @ w_in, cuts, axis=-1)
    hd = lambda a, d: a.reshape(B, T, ML_HEADS, d).astype(f32)
    gates = mg.astype(f32) + b_if.astype(f32)
    hm, ml_new = mlstm(hd(mq, ML_DK), hd(mk, ML_DK) * ML_DK ** -0.5, hd(mv, ML_DV),
                       gates[..., :ML_HEADS], jax.nn.log_sigmoid(gates[..., ML_HEADS:]), ml_state)
    hm = rms_norm(hm, ml_gain.reshape(ML_HEADS, ML_DV)).reshape(B, T, ML_WIDTH) * jax.nn.sigmoid(mo.astype(f32))
    G = len(SWA_GROUPS)
    grp = lambda a: a.reshape(B, T, G, SWA_HEADS, SWA_HD)
    aq = rotary(rms_norm(grp(aq), qn_g), pos)
    ak = rotary(rms_norm(grp(ak), kn_g), pos)
    av = grp(av)
    outs, lses, rows = [], [], []
    for g, (win, dil) in enumerate(SWA_GROUPS):
        qg, kg, vg = aq[:, :, g], ak[:, :, g], av[:, :, g]
        kv = jnp.stack([kg, vg], axis=2)
        if bufs is None:
            o, lse = dilated_prompt(qg, kg, vg, win, dil)
            rows.append(kv[:, T - min(win, T):])
        else:
            o, lse = dilated_sample(qg, kg, vg, bufs[g], win, dil)
            rows.append(kv)
        outs.append(o)
        lses.append(lse)
    wgt = jax.nn.softmax(jnp.stack(lses, axis=0), axis=0)
    ha = jnp.einsum('gbth,gbthd->bthd', wgt, jnp.stack(outs, axis=0)).reshape(B, T, SWA_HEADS * SWA_HD)
    y = jnp.concatenate([hm, ha], axis=-1).astype(h.dtype) @ w_out
    return y, ml_new, rows


def hgrn_chunk(S, inp):
    q, k, v, lf = inp
    L = q.shape[1]
    G = jnp.cumsum(lf, axis=1)
    causal = jnp.tril(jnp.ones((L, L), bool))
    diff = G[:, :, None] - G[:, None, :]
    dec = jnp.exp(jnp.where(causal[None, :, :, None, None], diff, -jnp.inf))
    A = jnp.einsum('bihc,bjhc,bijhc->bhij', q, k, dec)
    o = jnp.einsum('bhij,bjhv->bihv', A, v) + jnp.einsum('bihc,bhcv->bihv', q * jnp.exp(G), S)
    GL = G[:, -1]
    wk = k * jnp.exp(GL[:, None] - G)
    S_new = jnp.exp(GL)[..., None] * S + jnp.einsum('bjhc,bjhv->bhcv', wk, v)
    return S_new, o


def hgrn2(q, k, v, lf, S0):
    B, T = q.shape[:2]
    L = HG_CHUNK if T % HG_CHUNK == 0 else T
    nc = T // L
    chunks = lambda a: jnp.moveaxis(a.reshape((B, nc, L) + a.shape[2:]), 1, 0)
    S, o = lax.scan(hgrn_chunk, S0.astype(jnp.float32), tuple(chunks(a) for a in (q, k, v, lf)))
    return jnp.moveaxis(o, 0, 1).reshape(B, T, HG_HEADS, HG_DV), S


def odd_mixer(h, layer, w_in, lb_gamma, og_gain, w_out, S0):
    f32 = jnp.float32
    B, T, _ = h.shape
    FD, VD = HG_HEADS * HG_DK, HG_HEADS * HG_DV
    q, f, i, g = jnp.split(h @ w_in, [FD, 2 * FD, 2 * FD + VD], axis=-1)
    lb_all = jnp.cumsum(jax.nn.softmax(lb_gamma.astype(f32), axis=0), axis=0)
    lb = lb_all[layer] - lb_all[0]
    f = f.astype(f32)
    logf = jnp.logaddexp(jnp.log(lb), jnp.log1p(-lb) + jax.nn.log_sigmoid(f))
    kin = (1.0 - lb) * jax.nn.sigmoid(-f)
    hd = lambda a, d: a.reshape(B, T, HG_HEADS, d)
    o, S = hgrn2(hd(q.astype(f32), HG_DK), hd(kin, HG_DK), hd(i.astype(f32), HG_DV), hd(logf, HG_DK), S0)
    o = rms_norm(o, og_gain.reshape(HG_HEADS, HG_DV)).reshape(B, T, VD) * jax.nn.silu(g.astype(f32))
    return o.astype(h.dtype) @ w_out, S


def peer(h, w_q, sub_keys, u_tab, v_tab):
    B, T, D = h.shape
    ntok = B * T
    blk = min(PEER_BLOCK, ntok)
    nb = -(-ntok // blk)
    xt = jnp.pad(h.reshape(ntok, D), ((0, nb * blk - ntok), (0, 0))).reshape(nb, blk, D)

    def block_fn(xb):
        qb = (xb @ w_q).reshape(blk, PEER_HEADS, 2, PEER_DKEY // 2).astype(jnp.float32)
        s = jnp.einsum('tnhd,nhkd->tnhk', qb, sub_keys.astype(jnp.float32))
        sv, si = lax.top_k(s, PEER_TOPK)
        cand = sv[:, :, 0, :, None] + sv[:, :, 1, None, :]
        cidx = si[:, :, 0, :, None] * PEER_KEYS + si[:, :, 1, None, :]
        top_v, top_i = lax.top_k(cand.reshape(blk, PEER_HEADS, -1), PEER_TOPK)
        experts = jnp.take_along_axis(cidx.reshape(blk, PEER_HEADS, -1), top_i, axis=-1)
        gate = jax.nn.softmax(top_v, axis=-1)
        act = jax.nn.gelu(jnp.einsum('tnkd,td->tnk', u_tab[experts], xb).astype(jnp.float32), approximate=False)
        coef = (gate * act).astype(xb.dtype)
        return jnp.einsum('tnk,tnkd->td', coef, v_tab[experts])

    y = lax.map(block_fn, xt).reshape(nb * blk, D)[:ntok]
    return y.reshape(B, T, D)


def trunk(x, c, pos, p, st):
    f32 = jnp.float32
    B = x.shape[0]
    ml_C, ml_n, ml_m, hg_S = [], [], [], []
    rows = [[] for _ in SWA_GROUPS]
    for l in range(DEPTH):
        sh1, sc1, g1, sh2, sc2, g2 = ada_params(c, p['ada_w'][l], p['ada_b'][l])
        h = rms_norm(x, p['norm_g'][l, 0]) * (1 + sc1) + sh1
        if l % 2 == 0:
            e = l // 2
            if st is None:
                ml_state = (jnp.zeros((B, ML_HEADS, ML_DK, ML_DV), f32), jnp.zeros((B, ML_HEADS, ML_DK), f32), jnp.zeros((B, ML_HEADS), f32))
                bufs = None
            else:
                ml_state = (st['C'][e], st['n'][e], st['m'][e])
                bufs = [b[e] for b in st['swa']]
            y, (C, n, m), new_rows = even_mixer(h, pos, p['even_w_in'][e], p['even_b_if'][e], p['even_ml_gain'][e],
                                                 p['even_qn_g'][e], p['even_kn_g'][e], p['even_w_out'][e], ml_state, bufs)
            ml_C.append(C)
            ml_n.append(n)
            ml_m.append(m)
            for g in range(len(SWA_GROUPS)):
                rows[g].append(new_rows[g])
        else:
            o = l // 2
            S0 = jnp.zeros((B, HG_HEADS, HG_DK, HG_DV), f32) if st is None else st['S'][o]
            y, S = odd_mixer(h, l, p['odd_w_in'][o], p['odd_lb_gamma'], p['odd_og_gain'][o], p['odd_w_out'][o], S0)
            hg_S.append(S)
        x = x + (g1 * y).astype(x.dtype)
        h = rms_norm(x, p['norm_g'][l, 1]) * (1 + sc2) + sh2
        x = x + (g2 * peer(h, p['peer_w_q'][l], p['peer_sub_keys'][l], p['peer_u'][l], p['peer_v'][l])).astype(x.dtype)
    return (x, jnp.stack(ml_C), jnp.stack(ml_n), jnp.stack(ml_m), jnp.stack(rows[0]), jnp.stack(rows[1]), jnp.stack(rows[2]), jnp.stack(hg_S))


def setup_inputs(seed: int = 0) -> dict:
    key = jax.random.key(seed)
    ks = jax.random.split(key, 32)
    nrm = lambda k, shape, s: jax.random.normal(k, shape, jnp.float32) * s
    lw = [min(w, PAST_LEN) for w, _ in SWA_GROUPS]
    return {
        'x_prompt': nrm(ks[0], (BATCH, SEQ, D_MODEL), 1.0),
        'x_sample': nrm(ks[1], (DEC_BATCH, DEC_SEQ, D_MODEL), 1.0),
        'c_prompt': nrm(ks[2], (BATCH, D_MODEL), 1.0),
        'c_sample': nrm(ks[3], (DEC_BATCH, D_MODEL), 1.0),
        'state_mlstm_C': nrm(ks[4], (N_EVEN, DEC_BATCH, ML_HEADS, ML_DK, ML_DV), 0.05),
        'state_mlstm_n': jnp.abs(nrm(ks[5], (N_EVEN, DEC_BATCH, ML_HEADS, ML_DK), 0.05)),
        'state_mlstm_m': nrm(ks[6], (N_EVEN, DEC_BATCH, ML_HEADS), 1.0),
        'cache_swa_w128': nrm(ks[7], (N_EVEN, DEC_BATCH, lw[0], 2, SWA_HEADS, SWA_HD), 1.0),
        'cache_swa_w512': nrm(ks[8], (N_EVEN, DEC_BATCH, lw[1], 2, SWA_HEADS, SWA_HD), 1.0),
        'cache_swa_w2048': nrm(ks[9], (N_EVEN, DEC_BATCH, lw[2], 2, SWA_HEADS, SWA_HD), 1.0),
        'state_hgrn_S': nrm(ks[10], (N_ODD, DEC_BATCH, HG_HEADS, HG_DK, HG_DV), 0.3),
        'norm_g': 1.0 + nrm(ks[11], (DEPTH, 2, D_MODEL), 0.05),
        'ada_w': nrm(ks[12], (DEPTH, D_MODEL, 6 * D_MODEL), 0.5 * D_MODEL ** -0.5),
        'ada_b': nrm(ks[13], (DEPTH, 6 * D_MODEL), 0.02),
        'even_w_in': nrm(ks[14], (N_EVEN, D_MODEL, EVEN_IN), D_MODEL ** -0.5),
        'even_b_if': jnp.concatenate([nrm(ks[15], (N_EVEN, ML_HEADS), 0.1), 3.0 + nrm(ks[16], (N_EVEN, ML_HEADS), 0.5)], axis=-1),
        'even_ml_gain': 1.0 + nrm(ks[17], (N_EVEN, ML_WIDTH), 0.05),
        'even_qn_g': 1.0 + nrm(ks[18], (N_EVEN, SWA_HD), 0.05),
        'even_kn_g': 1.0 + nrm(ks[19], (N_EVEN, SWA_HD), 0.05),
        'even_w_out': nrm(ks[20], (N_EVEN, EVEN_OUT, D_MODEL), EVEN_OUT ** -0.5),
        'odd_w_in': nrm(ks[21], (N_ODD, D_MODEL, ODD_IN), D_MODEL ** -0.5),
        'odd_lb_gamma': nrm(ks[22], (DEPTH, HG_HEADS * HG_DK), 0.5),
        'odd_og_gain': 1.0 + nrm(ks[23], (N_ODD, HG_HEADS * HG_DV), 0.05),
        'odd_w_out': nrm(ks[24], (N_ODD, HG_HEADS * HG_DV, D_MODEL), (HG_HEADS * HG_DV) ** -0.5),
        'peer_w_q': nrm(ks[25], (DEPTH, D_MODEL, PEER_HEADS * PEER_DKEY), D_MODEL ** -0.5),
        'peer_sub_keys': nrm(ks[26], (DEPTH, PEER_HEADS, 2, PEER_KEYS, PEER_DKEY // 2), (PEER_DKEY // 2) ** -0.5),
        'peer_u': nrm(ks[27], (DEPTH, PEER_EXPERTS, D_MODEL), D_MODEL ** -0.5),
        'peer_v': nrm(ks[28], (DEPTH, PEER_EXPERTS, D_MODEL), PEER_HEADS ** -0.5),
    }


def reference(x_prompt, x_sample, c_prompt, c_sample, state_mlstm_C, state_mlstm_n, state_mlstm_m,
              cache_swa_w128, cache_swa_w512, cache_swa_w2048, state_hgrn_S, norm_g, ada_w, ada_b,
              even_w_in, even_b_if, even_ml_gain, even_qn_g, even_kn_g, even_w_out,
              odd_w_in, odd_lb_gamma, odd_og_gain, odd_w_out, peer_w_q, peer_sub_keys, peer_u, peer_v):
    p = {'norm_g': norm_g, 'ada_w': ada_w, 'ada_b': ada_b, 'even_w_in': even_w_in, 'even_b_if': even_b_if,
         'even_ml_gain': even_ml_gain, 'even_qn_g': even_qn_g, 'even_kn_g': even_kn_g, 'even_w_out': even_w_out,
         'odd_w_in': odd_w_in, 'odd_lb_gamma': odd_lb_gamma, 'odd_og_gain': odd_og_gain, 'odd_w_out': odd_w_out,
         'peer_w_q': peer_w_q, 'peer_sub_keys': peer_sub_keys, 'peer_u': peer_u, 'peer_v': peer_v}
    st = {'C': state_mlstm_C, 'n': state_mlstm_n, 'm': state_mlstm_m,
          'swa': (cache_swa_w128, cache_swa_w512, cache_swa_w2048), 'S': state_hgrn_S}
    pos_p = jnp.arange(x_prompt.shape[1])
    pos_s = PAST_LEN + jnp.arange(x_sample.shape[1])
    y_p, C_p, n_p, m_p, w128_p, w512_p, w2048_p, S_p = trunk(x_prompt, c_prompt, pos_p, p, None)
    y_s, C_s, n_s, m_s, w128_s, w512_s, w2048_s, S_s = trunk(x_sample, c_sample, pos_s, p, st)
    return (y_p, y_s, C_p, C_s, n_p, n_s, m_p, m_s, w128_p, w128_s, w512_p, w512_s, w2048_p, w2048_s, S_p, S_s)
```

```python
import functools

import jax
import jax.numpy as jnp
import numpy as np
from jax import lax
from jax.experimental import pallas as pl
from jax.experimental.pallas import tpu as pltpu

f32 = jnp.float32
bf16 = jnp.bfloat16

D_MODEL = 1024
PAST_LEN = 8192
ML_HEADS = 4
ML_DK = 128
ML_WIDTH = 512
ML_CHUNK = 128
SWA_GROUPS = ((128, 1), (512, 4), (2048, 16))
SWA_N = 128
SWA_HEADS = 4
SWA_HD = 64
SWA_WIDTH = 768
SWA_OUT = SWA_HEADS * SWA_HD
ROT_DIM = 16
ROPE_THETA = 500000.0
HG_HEADS = 8
HG_SUB = 16
PEER_KEYS = 128
PEER_EXPERTS = PEER_KEYS * PEER_KEYS
PEER_HEADS = 8
PEER_TOPK = 16
EPS = 1e-6
LANE = 128
TOK_CHUNK = 256
SAMPLE_PAD = 8
NEG = -1e30

EV_AQ, EV_AK, EV_AV = 0, 768, 1536
EV_MQ, EV_MK, EV_MV, EV_MO, EV_GATE = 18, 22, 26, 30, 34
EV_COLS = 35 * LANE


def _params(sem, vmem_mb):
    return pltpu.CompilerParams(dimension_semantics=sem, vmem_limit_bytes=vmem_mb * 1024 * 1024)


def _mod_spec(mod, m_rows, tm, grid_rank):
    nb, r, d = mod.shape
    if r == 1:
        tiles_per = (m_rows // nb) // tm
        if grid_rank == 1:
            return pl.BlockSpec((None, 1, d), lambda i: (i // tiles_per, 0, 0))
        return pl.BlockSpec((None, 1, d), lambda i, j: (i // tiles_per, 0, 0))
    assert nb == 1 and r == m_rows
    if grid_rank == 1:
        return pl.BlockSpec((None, tm, d), lambda i: (0, i, 0))
    return pl.BlockSpec((None, tm, d), lambda i, j: (0, i, 0))


def _ada_kernel(c_ref, w_ref, b_ref, o_ref):
    c = c_ref[...]
    s = (c * jax.nn.sigmoid(c)).astype(bf16)
    o_ref[...] = jnp.dot(s, w_ref[...].astype(bf16), preferred_element_type=f32) + b_ref[...]


def ada_all(c_all, ada_w, ada_b):
    depth, d, n = ada_w.shape
    bp = c_all.shape[0]
    tn = 1536
    return pl.pallas_call(
        _ada_kernel,
        grid=(depth, n // tn),
        in_specs=[pl.BlockSpec((bp, d), lambda l, j: (0, 0)),
                  pl.BlockSpec((None, d, tn), lambda l, j: (l, 0, j)),
                  pl.BlockSpec((None, 1, tn), lambda l, j: (l, 0, j))],
        out_specs=pl.BlockSpec((None, bp, tn), lambda l, j: (l, 0, j)),
        out_shape=jax.ShapeDtypeStruct((depth, bp, n), f32),
        compiler_params=_params(("arbitrary", "arbitrary"), 40),
        name="ada_mod",
    )(c_all, ada_w, ada_b.reshape(depth, 1, n))


def _nmm_kernel(x_ref, g_ref, sc_ref, sh_ref, w_ref, o_ref, *ht_ref):
    x = x_ref[...]
    y = x * lax.rsqrt(jnp.mean(x * x, axis=-1, keepdims=True) + EPS) * g_ref[...]
    hb = (y * (1.0 + sc_ref[...]) + sh_ref[...]).astype(bf16)
    o_ref[...] = jnp.dot(hb, w_ref[...], preferred_element_type=f32)
    if ht_ref:
        ht_ref[0][...] = hb.T


def norm_mod_matmul(x, g, sc, sh, w, *, emit_ht=False):
    m, d = x.shape
    n = w.shape[1]
    tm = TOK_CHUNK
    out_shape = [jax.ShapeDtypeStruct((m, n), f32)]
    out_specs = [pl.BlockSpec((tm, n), lambda i: (i, 0))]
    if emit_ht:
        out_shape.append(jax.ShapeDtypeStruct((m // tm, d, tm), bf16))
        out_specs.append(pl.BlockSpec((None, d, tm), lambda i: (i, 0, 0)))
    res = pl.pallas_call(
        _nmm_kernel,
        grid=(m // tm,),
        in_specs=[pl.BlockSpec((tm, d), lambda i: (i, 0)),
                  pl.BlockSpec((1, d), lambda i: (0, 0)),
                  _mod_spec(sc, m, tm, 1), _mod_spec(sh, m, tm, 1),
                  pl.BlockSpec((d, n), lambda i: (0, 0))],
        out_specs=out_specs,
        out_shape=out_shape,
        compiler_params=_params(("arbitrary",), 52),
        name="norm_mod_matmul",
    )(x, g.reshape(1, d), sc, sh, w)
    return res if emit_ht else res[0]


def _resid_kernel(n_y, x_ref, g_ref, *refs):
    ys, ws, o_ref = refs[:n_y], refs[n_y:2 * n_y], refs[2 * n_y]
    acc = None
    for y_ref, w_ref in zip(ys, ws):
        t = jnp.dot(y_ref[...].astype(bf16), w_ref[...], preferred_element_type=f32)
        acc = t if acc is None else acc + t
    o_ref[...] = x_ref[...] + g_ref[...] * acc


def resid_matmul(x, gate, ys, ws):
    m, d = x.shape
    tm = TOK_CHUNK
    n_y = len(ys)
    in_specs = [pl.BlockSpec((tm, d), lambda i: (i, 0)), _mod_spec(gate, m, tm, 1)]
    in_specs += [pl.BlockSpec((tm, y.shape[1]), lambda i: (i, 0)) for y in ys]
    in_specs += [pl.BlockSpec(w.shape, lambda i: (0, 0)) for w in ws]
    return pl.pallas_call(
        functools.partial(_resid_kernel, n_y),
        grid=(m // tm,),
        in_specs=in_specs,
        out_specs=pl.BlockSpec((tm, d), lambda i: (i, 0)),
        out_shape=jax.ShapeDtypeStruct((m, d), f32),
        compiler_params=_params(("arbitrary",), 40),
        name="resid_matmul",
    )(x, gate, *ys, *ws)


def _mlstm_kernel(valid, q_ref, k_ref, v_ref, o_ref, gt_ref, bif_ref, gain_ref, c0_ref, n0_ref, m0_ref,
                  hm_ref, c_ref, n_ref, m_ref):
    h = pl.program_id(1)
    ch = pl.program_id(2)
    length = q_ref.shape[0]

    @pl.when(ch == 0)
    def _():
        c_ref[...] = c0_ref[...]
        n_ref[...] = n0_ref[...]
        m_ref[...] = m0_ref[...]

    q = q_ref[...]
    k = k_ref[...] * (ML_DK ** -0.5)
    v = v_ref[...]
    gates = gt_ref[...] + bif_ref[...]
    lane = lax.broadcasted_iota(jnp.int32, gates.shape, 1)
    ig = jnp.sum(jnp.where(lane == h, gates, 0.0), axis=1, keepdims=True)
    fg = jnp.sum(jnp.where(lane == h + ML_HEADS, gates, 0.0), axis=1, keepdims=True)
    lf = jax.nn.log_sigmoid(fg)
    if valid < length:
        tok = lax.broadcasted_iota(jnp.int32, (length, 1), 0)
        ig = jnp.where(tok < valid, ig, NEG)
        lf = jnp.where(tok < valid, lf, 0.0)
    row = lax.broadcasted_iota(jnp.int32, (length, length), 0)
    col = lax.broadcasted_iota(jnp.int32, (length, length), 1)
    causal = row >= col
    tril = jnp.where(causal, 1.0, 0.0)
    fb = jnp.dot(tril, jnp.broadcast_to(lf, (length, LANE)), preferred_element_type=f32,
                 precision=lax.Precision.HIGHEST)
    fcol = fb[:, :1]
    fsq = fb[:, :length]
    rowv = jnp.sum(jnp.where(row == col, fsq - ig, 0.0), axis=0, keepdims=True)
    log_d = jnp.where(causal, fsq - rowv, -jnp.inf)
    m_prev = m_ref[...][:, :1]
    inter = fcol + m_prev
    m_loc = jnp.maximum(inter, jnp.max(log_d, axis=1, keepdims=True))
    w_intra = jnp.exp(log_d - m_loc)
    w_inter = jnp.exp(inter - m_loc)
    qb = q.astype(bf16)
    kb = k.astype(bf16)
    vb = v.astype(bf16)
    s = lax.dot_general(qb, kb, (((1,), (1,)), ((), ())), preferred_element_type=f32) * w_intra
    c_prev = c_ref[...]
    n_prev = n_ref[...]
    num = (jnp.dot(s.astype(bf16), vb, preferred_element_type=f32)
           + w_inter * jnp.dot(qb, c_prev.astype(bf16), preferred_element_type=f32))
    den = jnp.sum(s, axis=1, keepdims=True) + w_inter * jnp.sum(q * n_prev, axis=1, keepdims=True)
    den = jnp.maximum(jnp.abs(den), jnp.exp(-m_loc))
    hh = num / den
    hn = hh * lax.rsqrt(jnp.mean(hh * hh, axis=1, keepdims=True) + EPS) * gain_ref[...]
    hm_ref[...] = hn * jax.nn.sigmoid(o_ref[...])
    f_last = fcol[length - 1:length, :]
    logw = f_last - fcol + ig
    m_new = jnp.maximum(f_last + m_prev, jnp.max(logw, axis=0, keepdims=True))
    wk = jnp.exp(logw - m_new)
    decay = jnp.exp(f_last + m_prev - m_new)
    kw = k * wk
    c_ref[...] = decay * c_prev + lax.dot_general(kw.astype(bf16), vb, (((0,), (0,)), ((), ())),
                                                  preferred_element_type=f32)
    n_ref[...] = decay * n_prev + jnp.sum(kw, axis=0, keepdims=True)
    m_ref[...] = jnp.broadcast_to(m_new, m_ref.shape)


def mlstm(proj, b_if, ml_gain, c0, n0, m0, *, nbatch, chunk, valid):
    m = proj.shape[0]
    nc = (m // nbatch) // chunk
    bif = jnp.zeros((1, LANE), f32).at[0, :2 * ML_HEADS].set(b_if.astype(f32))
    tok = lambda off: pl.BlockSpec((chunk, LANE), lambda b, h, c: (b * nc + c, off + h))
    st_c = pl.BlockSpec((None, None, ML_DK, LANE), lambda b, h, c: (b, h, 0, 0))
    st_v = pl.BlockSpec((None, None, 1, LANE), lambda b, h, c: (b, h, 0, 0))
    return pl.pallas_call(
        functools.partial(_mlstm_kernel, valid),
        grid=(nbatch, ML_HEADS, nc),
        in_specs=[tok(EV_MQ), tok(EV_MK), tok(EV_MV), tok(EV_MO),
                  pl.BlockSpec((chunk, LANE), lambda b, h, c: (b * nc + c, EV_GATE)),
                  pl.BlockSpec((1, LANE), lambda b, h, c: (0, 0)),
                  pl.BlockSpec((1, LANE), lambda b, h, c: (0, h)),
                  st_c, st_v, st_v],
        out_specs=[pl.BlockSpec((chunk, LANE), lambda b, h, c: (b * nc + c, h)), st_c, st_v, st_v],
        out_shape=[jax.ShapeDtypeStruct((m, ML_WIDTH), f32),
                   jax.ShapeDtypeStruct(c0.shape, f32), jax.ShapeDtypeStruct(n0.shape, f32),
                   jax.ShapeDtypeStruct(m0.shape, f32)],
        compiler_params=_params(("arbitrary", "arbitrary", "arbitrary"), 32),
        name="mlstm",
    )(proj, proj, proj, proj, proj, bif, ml_gain.reshape(1, ML_WIDTH), c0, n0, m0)


def _qkprep_kernel(q_ref, k_ref, v_ref, qg_ref, kg_ref, ta_ref, tb_ref, tc_ref, bd_ref, qo_ref, ko_ref, vo_ref):
    ta, tb, tc, bd = ta_ref[...], tb_ref[...], tc_ref[...], bd_ref[...]
    for src, dst, gain_ref in ((q_ref, qo_ref, qg_ref), (k_ref, ko_ref, kg_ref)):
        gain = gain_ref[...]
        for blk in range(SWA_WIDTH // LANE):
            x = src[:, blk * LANE:(blk + 1) * LANE]
            ms = jnp.dot(x * x, bd, preferred_element_type=f32, precision=lax.Precision.HIGHEST)
            xn = x * lax.rsqrt(ms + EPS) * gain
            rot = xn * ta + pltpu.roll(xn, LANE - ROT_DIM // 2, 1) * tb + pltpu.roll(xn, ROT_DIM // 2, 1) * tc
            dst[:, blk * LANE:(blk + 1) * LANE] = rot
    vo_ref[...] = v_ref[...]


def _rotary_tables(pos):
    half = ROT_DIM // 2
    inv = ROPE_THETA ** (-jnp.arange(half, dtype=f32) / half)
    ang = pos.astype(f32)[:, None] * inv
    cos, sin = jnp.cos(ang), jnp.sin(ang)
    t = pos.shape[0]
    one = jnp.ones((t, SWA_HD - ROT_DIM), f32)
    zero = jnp.zeros((t, SWA_HD - ROT_DIM), f32)
    zh = jnp.zeros((t, half), f32)
    a = jnp.concatenate([cos, cos, one], axis=1)
    b = jnp.concatenate([-sin, zh, zero], axis=1)
    c = jnp.concatenate([zh, sin, zero], axis=1)
    return tuple(jnp.concatenate([u, u], axis=1) for u in (a, b, c))


def qk_prep(proj, qn_g, kn_g, tables):
    m = proj.shape[0]
    tm = TOK_CHUNK
    ttab = tables[0].shape[0]
    ntab = ttab // tm
    bd = np.kron(np.eye(LANE // SWA_HD), np.full((SWA_HD, SWA_HD), 1.0 / SWA_HD)).astype(np.float32)
    gain = lambda g: jnp.tile(g.astype(f32), LANE // SWA_HD).reshape(1, LANE)
    col = lambda blk: pl.BlockSpec((tm, SWA_WIDTH), lambda i: (i, blk))
    tab = pl.BlockSpec((tm, LANE), lambda i: (i % ntab, 0))
    one = pl.BlockSpec((1, LANE), lambda i: (0, 0))
    out = pl.BlockSpec((tm, SWA_WIDTH), lambda i: (i, 0))
    return pl.pallas_call(
        _qkprep_kernel,
        grid=(m // tm,),
        in_specs=[col(0), col(1), col(2), one, one, tab, tab, tab, pl.BlockSpec((LANE, LANE), lambda i: (0, 0))],
        out_specs=[out, out, out],
        out_shape=[jax.ShapeDtypeStruct((m, SWA_WIDTH), f32)] * 3,
        compiler_params=_params(("arbitrary",), 32),
        name="qk_prep",
    )(proj, proj, proj, gain(qn_g), gain(kn_g), *tables, jnp.asarray(bd))


def _swa_kernel(q_ref, kc_ref, kp_ref, vc_ref, vp_ref, o_ref, l_ref):
    c = pl.program_id(3)
    n = SWA_N
    q = q_ref[...]
    kk = jnp.concatenate([kp_ref[...], kc_ref[...]], axis=0).astype(bf16)
    vv = jnp.concatenate([vp_ref[...], vc_ref[...]], axis=0).astype(bf16)
    a = lax.broadcasted_iota(jnp.int32, (n, 2 * n), 0)
    ko = lax.broadcasted_iota(jnp.int32, (n, 2 * n), 1)
    mask = (ko >= a) & (ko <= a + n) & ((c > 0) | (ko >= n))
    lane = lax.broadcasted_iota(jnp.int32, (n, LANE), 1)
    out = jnp.zeros((n, LANE), f32)
    lse = jnp.zeros((n, LANE), f32)
    for hh in range(LANE // SWA_HD):
        head = (lane >= hh * SWA_HD) & (lane < (hh + 1) * SWA_HD)
        qh = jnp.where(head, q, 0.0).astype(bf16)
        s = lax.dot_general(qh, kk, (((1,), (1,)), ((), ())), preferred_element_type=f32) * (SWA_HD ** -0.5)
        s = jnp.where(mask, s, -jnp.inf)
        mx = jnp.max(s, axis=1, keepdims=True)
        p = jnp.exp(s - mx)
        den = jnp.sum(p, axis=1, keepdims=True)
        o = jnp.dot(p.astype(bf16), vv, preferred_element_type=f32) / den
        out = jnp.where(head, o, out)
        lse = jnp.where(head, mx + jnp.log(den), lse)
    o_ref[...] = out
    l_ref[...] = lse


def swa_prompt(aq, ak, av, group, *, nbatch):
    m = aq.shape[0]
    win, dil = SWA_GROUPS[group]
    n = SWA_N
    t = m // nbatch
    nb = t // (dil * n)
    rows = m // dil
    nblk = SWA_WIDTH // LANE
    view = lambda x: x.reshape(rows, dil * SWA_WIDTH)
    cur = pl.BlockSpec((n, LANE), lambda b, r, p, c: (b * nb + c, r * nblk + group * 2 + p))
    prev = pl.BlockSpec((n, LANE), lambda b, r, p, c: (b * nb + jnp.maximum(c - 1, 0), r * nblk + group * 2 + p))
    outs = pl.BlockSpec((n, LANE), lambda b, r, p, c: (b * nb + c, r * 2 + p))
    o, l = pl.pallas_call(
        _swa_kernel,
        grid=(nbatch, dil, 2, nb),
        in_specs=[cur, cur, prev, cur, prev],
        out_specs=[outs, outs],
        out_shape=[jax.ShapeDtypeStruct((rows, dil * SWA_OUT), f32)] * 2,
        compiler_params=_params(("arbitrary",) * 4, 32),
        name=f"swa_prompt_w{win}",
    )(view(aq), view(ak), view(ak), view(av), view(av))
    return o.reshape(m, SWA_OUT), l.reshape(m, SWA_OUT)


def _merge_kernel(o0, o1, o2, l0, l1, l2, out_ref):
    ls = (l0[...], l1[...], l2[...])
    mx = jnp.maximum(jnp.maximum(ls[0], ls[1]), ls[2])
    es = [jnp.exp(l - mx) for l in ls]
    tot = es[0] + es[1] + es[2]
    out_ref[...] = (es[0] * o0[...] + es[1] * o1[...] + es[2] * o2[...]) / tot


def swa_merge(outs, lses):
    m = outs[0].shape[0]
    tm = 512
    spec = pl.BlockSpec((tm, SWA_OUT), lambda i: (i, 0))
    return pl.pallas_call(
        _merge_kernel, grid=(m // tm,), in_specs=[spec] * 6, out_specs=spec,
        out_shape=jax.ShapeDtypeStruct((m, SWA_OUT), f32),
        compiler_params=_params(("arbitrary",), 32), name="swa_merge",
    )(*outs, *lses)


def _swa_sample_kernel(nvalid, q_ref, k_ref, v_ref, b0_ref, b1_ref, b2_ref, bd_ref, o_ref):
    bd = bd_ref[...]
    n = SWA_N
    bufs = (b0_ref, b1_ref, b2_ref)
    arow = lax.broadcasted_iota(jnp.int32, (n, SWA_OUT), 0)
    nrow = lax.broadcasted_iota(jnp.int32, (SAMPLE_PAD, SWA_OUT), 0)
    rows = []
    for l in range(nvalid):
        outs, lses = [], []
        for g, (win, dil) in enumerate(SWA_GROUPS):
            cols = slice(g * SWA_OUT, (g + 1) * SWA_OUT)
            ql = q_ref[l:l + 1, cols]
            kn, vn = k_ref[:, cols], v_ref[:, cols]
            r = l % dil
            kb = bufs[g][:, r * 2 * SWA_OUT:r * 2 * SWA_OUT + SWA_OUT]
            vb = bufs[g][:, r * 2 * SWA_OUT + SWA_OUT:(r + 1) * 2 * SWA_OUT]
            scale = SWA_HD ** -0.5
            sb = jnp.dot(kb * ql, bd, preferred_element_type=f32, precision=lax.Precision.HIGHEST) * scale
            sn = jnp.dot(kn * ql, bd, preferred_element_type=f32, precision=lax.Precision.HIGHEST) * scale
            sb = jnp.where(arow * dil + r >= l, sb, -jnp.inf)
            ok_new = (nrow <= l) & (nrow < nvalid) & ((l - nrow) % dil == 0)
            sn = jnp.where(ok_new, sn, -jnp.inf)
            mx = jnp.maximum(jnp.max(sb, axis=0, keepdims=True), jnp.max(sn, axis=0, keepdims=True))
            pb, pn = jnp.exp(sb - mx), jnp.exp(sn - mx)
            den = jnp.sum(pb, axis=0, keepdims=True) + jnp.sum(pn, axis=0, keepdims=True)
            o = (jnp.sum(pb * vb, axis=0, keepdims=True) + jnp.sum(pn * vn, axis=0, keepdims=True)) / den
            outs.append(o)
            lses.append(mx + jnp.log(den))
        mx = jnp.maximum(jnp.maximum(lses[0], lses[1]), lses[2])
        es = [jnp.exp(x - mx) for x in lses]
        rows.append((es[0] * outs[0] + es[1] * outs[1] + es[2] * outs[2]) / (es[0] + es[1] + es[2]))
    rows.append(jnp.zeros((SAMPLE_PAD - nvalid, SWA_OUT), f32))
    o_ref[...] = jnp.concatenate(rows, axis=0)


def swa_sample(aq, ak, av, caches, *, nbatch, nvalid):
    m = aq.shape[0]
    bd = np.kron(np.eye(SWA_HEADS), np.ones((SWA_HD, SWA_HD))).astype(np.float32)
    tok = pl.BlockSpec((SAMPLE_PAD, SWA_WIDTH), lambda b: (b, 0))
    views, specs = [], []
    for (win, dil), buf in zip(SWA_GROUPS, caches):
        assert buf.shape[1] == win and nvalid <= dil * 4 and nvalid <= SWA_N
        views.append(buf.reshape(nbatch, win // dil, dil * 2 * SWA_OUT))
        width = min(dil, nvalid) * 2 * SWA_OUT
        specs.append(pl.BlockSpec((None, SWA_N, width), lambda b: (b, 0, 0)))
    return pl.pallas_call(
        functools.partial(_swa_sample_kernel, nvalid),
        grid=(nbatch,),
        in_specs=[tok, tok, tok, *specs, pl.BlockSpec((SWA_OUT, SWA_OUT), lambda b: (0, 0))],
        out_specs=pl.BlockSpec((SAMPLE_PAD, SWA_OUT), lambda b: (b, 0)),
        out_shape=jax.ShapeDtypeStruct((m, SWA_OUT), f32),
        compiler_params=_params(("arbitrary",), 32),
        name="swa_sample",
    )(aq, ak, av, *views, jnp.asarray(bd))


def _hgrn_kernel(layer, sub, valid, q_ref, f_ref, i_ref, g_ref, gam_ref, gain_ref, s0_ref, o_ref, s_ref, st_ref):
    blk = pl.program_id(2)
    nblk = pl.num_programs(2)
    length = q_ref.shape[0]

    @pl.when(blk == 0)
    def _():
        st_ref[...] = s0_ref[...].T

    sm = jax.nn.softmax(gam_ref[...], axis=0)
    run = sm[0:1]
    for kk in range(1, layer + 1):
        run = run + sm[kk:kk + 1]
    lb = run - sm[0:1]
    log_lb, log_1m = jnp.log(lb), jnp.log1p(-lb)
    gain = gain_ref[...]
    r = lax.broadcasted_iota(jnp.int32, (sub, sub), 0)
    c = lax.broadcasted_iota(jnp.int32, (sub, sub), 1)
    tril = jnp.where(r >= c, 1.0, 0.0)
    jrow = lax.broadcasted_iota(jnp.int32, (sub, LANE), 0)
    jlane = lax.broadcasted_iota(jnp.int32, (sub, LANE), 1)
    ones = jnp.ones((LANE, LANE), bf16)
    for sc in range(length // sub):
        rows = slice(sc * sub, (sc + 1) * sub)
        q, fr, v = q_ref[rows, :], f_ref[rows, :], i_ref[rows, :]
        lg = jnp.logaddexp(log_lb, log_1m + jax.nn.log_sigmoid(fr))
        kin = (1.0 - lb) * jax.nn.sigmoid(-fr)
        if valid < length:
            tok = lax.broadcasted_iota(jnp.int32, (sub, 1), 0) + sc * sub
            lg = jnp.where(tok < valid, lg, 0.0)
            kin = jnp.where(tok < valid, kin, 0.0)
        gc = jnp.dot(tril, lg, preferred_element_type=f32, precision=lax.Precision.HIGHEST)
        ge = gc[sub - 1:sub, :]
        st = st_ref[...]
        o_inter = lax.dot_general((q * jnp.exp(gc)).astype(bf16), st.astype(bf16), (((1,), (1,)), ((), ())),
                                  preferred_element_type=f32)
        prods = []
        for i in range(sub):
            dec = jnp.exp(jnp.minimum(gc[i:i + 1, :] - gc, 0.0))
            prods.append(dec * kin * q[i:i + 1, :])
        red = jnp.dot(jnp.concatenate(prods, axis=0).astype(bf16), ones, preferred_element_type=f32)
        arows = []
        for i in range(sub):
            pick = (jlane == jrow) & (jrow <= i)
            arows.append(jnp.sum(jnp.where(pick, red[i * sub:(i + 1) * sub, :], 0.0), axis=0, keepdims=True))
        amat = jnp.concatenate(arows, axis=0)[:, :sub]
        o = o_inter + jnp.dot(amat, v, preferred_element_type=f32)
        kd = kin * jnp.exp(ge - gc)
        st_ref[...] = st * jnp.exp(ge) + lax.dot_general(v.astype(bf16), kd.astype(bf16), (((0,), (0,)), ((), ())),
                                                         preferred_element_type=f32)
        on = o * lax.rsqrt(jnp.mean(o * o, axis=1, keepdims=True) + EPS) * gain
        gg = g_ref[rows, :]
        o_ref[rows, :] = on * (gg * jax.nn.sigmoid(gg))

    @pl.when(blk == nblk - 1)
    def _():
        s_ref[...] = st_ref[...].T


def hgrn2(proj, lb_gamma, og_gain, s0, *, layer, nbatch, block, sub, valid):
    m = proj.shape[0]
    nblk = (m // nbatch) // block
    depth = lb_gamma.shape[0]
    tok = lambda off: pl.BlockSpec((block, LANE), lambda b, h, c: (b * nblk + c, off + h))
    st = pl.BlockSpec((None, None, LANE, LANE), lambda b, h, c: (b, h, 0, 0))
    return pl.pallas_call(
        functools.partial(_hgrn_kernel, layer, sub, valid),
        grid=(nbatch, HG_HEADS, nblk),
        in_specs=[tok(0), tok(HG_HEADS), tok(2 * HG_HEADS), tok(3 * HG_HEADS),
                  pl.BlockSpec((depth, LANE), lambda b, h, c: (0, h)),
                  pl.BlockSpec((1, LANE), lambda b, h, c: (0, h)), st],
        out_specs=[pl.BlockSpec((block, LANE), lambda b, h, c: (b * nblk + c, h)), st],
        out_shape=[jax.ShapeDtypeStruct((m, D_MODEL), f32), jax.ShapeDtypeStruct(s0.shape, f32)],
        scratch_shapes=[pltpu.VMEM((LANE, LANE), f32)],
        compiler_params=_params(("arbitrary",) * 3, 32),
        name="hgrn2",
    )(proj, proj, proj, proj, lb_gamma, og_gain.reshape(1, D_MODEL), s0)


def _cand_pairs():
    return [(i, j) for i in range(PEER_TOPK) for j in range(PEER_TOPK) if (i + 1) * (j + 1) <= PEER_TOPK]


def _top_rows(x, count):
    rows, cur = [], x
    for _ in range(count):
        mx = jnp.max(cur, axis=0, keepdims=True)
        rows.append(mx)
        cur = jnp.where(cur == mx, -jnp.inf, cur)
    return rows


def _route_kernel(q_ref, keys_ref, thr_ref, wa_ref, b_ref, eb_ref):
    pairs = _cand_pairs()
    pad = (-len(pairs)) % 8
    for h in range(PEER_HEADS):
        qh = q_ref[:, h * LANE:(h + 1) * LANE].astype(bf16)
        nt = (((1,), (1,)), ((), ()))
        a = lax.dot_general(keys_ref[2 * h], qh, nt, preferred_element_type=f32)
        b = lax.dot_general(keys_ref[2 * h + 1], qh, nt, preferred_element_type=f32)
        ta, tb = _top_rows(a, PEER_TOPK), _top_rows(b, PEER_TOPK)
        cand = jnp.concatenate([ta[i] + tb[j] for i, j in pairs]
                               + [jnp.full((pad, a.shape[1]), -jnp.inf, f32)], axis=0)
        tau = _top_rows(cand, PEER_TOPK)[-1]
        top = ta[0] + tb[0]
        z = jnp.sum(jnp.where(cand >= tau, jnp.exp(cand - top), 0.0), axis=0, keepdims=True)
        rows = slice(h * PEER_KEYS, (h + 1) * PEER_KEYS)
        thr_ref[rows, :] = tau - a
        wa_ref[rows, :] = jnp.exp(a - ta[0]) / z
        b_ref[rows, :] = b
        eb_ref[rows, :] = jnp.exp(b - tb[0])


def peer_route(q, keys_pad):
    m = q.shape[0]
    tm = TOK_CHUNK
    out = pl.BlockSpec((None, PEER_HEADS * PEER_KEYS, tm), lambda i: (i, 0, 0))
    return pl.pallas_call(
        _route_kernel,
        grid=(m // tm,),
        in_specs=[pl.BlockSpec((tm, D_MODEL), lambda i: (i, 0)),
                  pl.BlockSpec(keys_pad.shape, lambda i: (0, 0, 0))],
        out_specs=[out] * 4,
        out_shape=[jax.ShapeDtypeStruct((m // tm, PEER_HEADS * PEER_KEYS, tm), f32)] * 4,
        compiler_params=_params(("arbitrary",), 40),
        name="peer_route",
    )(q, keys_pad)


def _gelu(x):
    return 0.5 * x * (1.0 + lax.erf(x * (2.0 ** -0.5)))


def _peer_dense_kernel(ht_ref, u_ref, vt_ref, thr_ref, wa_ref, b_ref, eb_ref, x_ref, g_ref, o_ref, acc_ref, coef_ref):
    j = pl.program_id(1)
    nj = pl.num_programs(1)
    nchunk = ht_ref.shape[0]
    te = u_ref.shape[0]
    ni = te // PEER_KEYS

    @pl.when(j == 0)
    def _():
        acc_ref[...] = jnp.zeros_like(acc_ref)

    for tc in range(nchunk):
        act = _gelu(jnp.dot(u_ref[...], ht_ref[tc], preferred_element_type=f32))
        for ii in range(ni):
            i = j * ni + ii
            gate = jnp.zeros((PEER_KEYS, TOK_CHUNK), f32)
            for h in range(PEER_HEADS):
                thr = thr_ref[tc, pl.ds(h * PEER_KEYS + i, 1), :]
                wa = wa_ref[tc, pl.ds(h * PEER_KEYS + i, 1), :]
                rows = slice(h * PEER_KEYS, (h + 1) * PEER_KEYS)
                gate = gate + jnp.where(b_ref[tc, rows, :] >= thr, eb_ref[tc, rows, :], 0.0) * wa
            coef_ref[ii * PEER_KEYS:(ii + 1) * PEER_KEYS, :] = (gate * act[ii * PEER_KEYS:(ii + 1) * PEER_KEYS, :]).astype(bf16)
        acc_ref[tc] += jnp.dot(vt_ref[...], coef_ref[...], preferred_element_type=f32)

    @pl.when(j == nj - 1)
    def _():
        for tc in range(nchunk):
            rows = slice(tc * TOK_CHUNK, (tc + 1) * TOK_CHUNK)
            gate = g_ref[...] if g_ref.shape[0] == 1 else g_ref[rows, :]
            o_ref[rows, :] = x_ref[rows, :] + gate * acc_ref[tc].T


def peer_dense(ht, u, vt, route, x, gate):
    m, d = x.shape
    nchunk_total = ht.shape[0]
    tm = min(512, m)
    te = 512
    cpt = tm // TOK_CHUNK
    rt = pl.BlockSpec((cpt, PEER_HEADS * PEER_KEYS, TOK_CHUNK), lambda i, j: (i, 0, 0))
    assert nchunk_total * TOK_CHUNK == m
    return pl.pallas_call(
        _peer_dense_kernel,
        grid=(m // tm, PEER_EXPERTS // te),
        in_specs=[pl.BlockSpec((cpt, d, TOK_CHUNK), lambda i, j: (i, 0, 0)),
                  pl.BlockSpec((te, d), lambda i, j: (j, 0)),
                  pl.BlockSpec((d, te), lambda i, j: (0, j)),
                  rt, rt, rt, rt,
                  pl.BlockSpec((tm, d), lambda i, j: (i, 0)),
                  _mod_spec(gate, m, tm, 2)],
        out_specs=pl.BlockSpec((tm, d), lambda i, j: (i, 0)),
        out_shape=jax.ShapeDtypeStruct((m, d), f32),
        scratch_shapes=[pltpu.VMEM((cpt, d, TOK_CHUNK), f32), pltpu.VMEM((te, TOK_CHUNK), bf16)],
        compiler_params=_params(("arbitrary", "arbitrary"), 56),
        name="peer_dense",
    )(ht, u, vt, *route, x, gate)


def peer_layer(x, g, sc, sh, gate, w_q, keys_pad, u, vt):
    q, ht = norm_mod_matmul(x, g, sc, sh, w_q, emit_ht=True)
    route = peer_route(q, keys_pad)
    return peer_dense(ht, u, vt, route, x, gate)


def _prep_weights(p):
    w = {}
    ev = p['even_w_in'][0]
    cuts = np.cumsum([ML_WIDTH] * 4 + [2 * ML_HEADS] + [SWA_WIDTH] * 2)
    mq_mo, mg, att = ev[:, :cuts[3]], ev[:, cuts[3]:cuts[4]], ev[:, cuts[4]:]
    pad = jnp.zeros((D_MODEL, LANE - 2 * ML_HEADS), ev.dtype)
    w['even_in'] = jnp.concatenate([att, mq_mo, mg, pad], axis=1).astype(bf16)
    wo = p['even_w_out'][0].astype(bf16)
    w['even_out'] = (wo[:ML_WIDTH], wo[ML_WIDTH:])
    w['odd_in'] = p['odd_w_in'][0].astype(bf16)
    w['odd_out'] = p['odd_w_out'][0].astype(bf16)
    w['peer_q'] = p['peer_w_q'].astype(bf16)
    sk = p['peer_sub_keys'].astype(bf16)
    zero = jnp.zeros_like(sk)
    half0 = jnp.concatenate([sk[:, :, 0], zero[:, :, 0]], axis=-1)
    half1 = jnp.concatenate([zero[:, :, 1], sk[:, :, 1]], axis=-1)
    depth = sk.shape[0]
    w['peer_keys'] = jnp.stack([half0, half1], axis=2).reshape(depth, 2 * PEER_HEADS, PEER_KEYS, LANE)
    w['peer_u'] = p['peer_u'].astype(bf16)
    w['peer_vt'] = jnp.swapaxes(p['peer_v'], 1, 2).astype(bf16)
    return w


def _trunk(x, mods, tables, p, w, st, *, nbatch, chunk, valid, hg_block, hg_sub):
    m = x.shape[0]
    res = {}
    sh1, sc1, g1, sh2, sc2, g2 = mods[0]
    proj = norm_mod_matmul(x, p['norm_g'][0, 0], sc1, sh1, w['even_in'])
    if st is None:
        c0 = jnp.zeros((nbatch, ML_HEADS, ML_DK, LANE), f32)
        n0 = jnp.zeros((nbatch, ML_HEADS, 1, LANE), f32)
        m0 = jnp.zeros((nbatch, ML_HEADS, 1, LANE), f32)
    else:
        c0 = st['C'][0].astype(f32)
        n0 = st['n'][0].astype(f32)[:, :, None, :]
        m0 = jnp.broadcast_to(st['m'][0].astype(f32)[:, :, None, None], (nbatch, ML_HEADS, 1, LANE))
    hm, c_new, n_new, m_new = mlstm(proj, p['even_b_if'][0], p['even_ml_gain'][0], c0, n0, m0,
                                    nbatch=nbatch, chunk=chunk, valid=valid)
    res['C'], res['n'], res['m'] = c_new, n_new[:, :, 0, :], m_new[:, :, 0, 0]
    aq, ak, av = qk_prep(proj, p['even_qn_g'][0], p['even_kn_g'][0], tables)
    if st is None:
        outs, lses = zip(*[swa_prompt(aq, ak, av, g, nbatch=nbatch) for g in range(len(SWA_GROUPS))])
        ha = swa_merge(outs, lses)
    else:
        ha = swa_sample(aq, ak, av, [b[0] for b in st['swa']], nbatch=nbatch, nvalid=valid)
    res['kv'] = (ak, av)
    x = resid_matmul(x, g1, [hm, ha], list(w['even_out']))
    x = peer_layer(x, p['norm_g'][0, 1], sc2, sh2, g2, w['peer_q'][0], w['peer_keys'][0], w['peer_u'][0], w['peer_vt'][0])
    sh1, sc1, g1, sh2, sc2, g2 = mods[1]
    proj = norm_mod_matmul(x, p['norm_g'][1, 0], sc1, sh1, w['odd_in'])
    s0 = jnp.zeros((nbatch, HG_HEADS, LANE, LANE), f32) if st is None else st['S'][0].astype(f32)
    ho, s_new = hgrn2(proj, p['odd_lb_gamma'], p['odd_og_gain'][0], s0, layer=1, nbatch=nbatch,
                      block=hg_block, sub=hg_sub, valid=valid)
    res['S'] = s_new
    x = resid_matmul(x, g1, [ho], [w['odd_out']])
    x = peer_layer(x, p['norm_g'][1, 1], sc2, sh2, g2, w['peer_q'][1], w['peer_keys'][1], w['peer_u'][1], w['peer_vt'][1])
    res['y'] = x
    return res


def kernel(x_prompt, x_sample, c_prompt, c_sample, state_mlstm_C, state_mlstm_n, state_mlstm_m, cache_swa_w128, cache_swa_w512, cache_swa_w2048, state_hgrn_S, norm_g, ada_w, ada_b, even_w_in, even_b_if, even_ml_gain, even_qn_g, even_kn_g, even_w_out, odd_w_in, odd_lb_gamma, odd_og_gain, odd_w_out, peer_w_q, peer_sub_keys, peer_u, peer_v):
    p = {'norm_g': norm_g, 'even_w_in': even_w_in, 'even_b_if': even_b_if, 'even_ml_gain': even_ml_gain,
         'even_qn_g': even_qn_g, 'even_kn_g': even_kn_g, 'even_w_out': even_w_out, 'odd_w_in': odd_w_in,
         'odd_lb_gamma': odd_lb_gamma, 'odd_og_gain': odd_og_gain, 'odd_w_out': odd_w_out, 'peer_w_q': peer_w_q,
         'peer_sub_keys': peer_sub_keys, 'peer_u': peer_u, 'peer_v': peer_v}
    w = _prep_weights(p)
    bp, tp, d = x_prompt.shape
    bs, ts, _ = x_sample.shape
    depth = ada_w.shape[0]
    assert tp % (SWA_GROUPS[-1][0]) == 0 and ts <= SAMPLE_PAD // 2 and (bs * SAMPLE_PAD) % TOK_CHUNK == 0

    nrows = -(-(bp + bs) // 8) * 8
    c_all = jnp.zeros((nrows, d), f32).at[:bp].set(c_prompt).at[bp:bp + bs].set(c_sample)
    mod = ada_all(c_all, ada_w, ada_b)
    split = lambda a: [a[..., k * d:(k + 1) * d] for k in range(6)]
    mods_p = [[u[:, None, :] for u in split(mod[l, :bp])] for l in range(depth)]
    mods_s = [[jnp.repeat(u, SAMPLE_PAD, axis=0)[None] for u in split(mod[l, bp:bp + bs])] for l in range(depth)]

    tables_p = _rotary_tables(jnp.arange(tp))
    rp = _trunk(x_prompt.reshape(bp * tp, d), mods_p, tables_p, p, w, None,
                nbatch=bp, chunk=ML_CHUNK, valid=ML_CHUNK, hg_block=ML_CHUNK, hg_sub=HG_SUB)

    xs = jnp.zeros((bs, SAMPLE_PAD, d), f32).at[:, :ts].set(x_sample).reshape(bs * SAMPLE_PAD, d)
    pos_s = jnp.tile(PAST_LEN + jnp.arange(SAMPLE_PAD), bs)
    tables_s = _rotary_tables(pos_s)
    st = {'C': state_mlstm_C, 'n': state_mlstm_n, 'm': state_mlstm_m,
          'swa': (cache_swa_w128, cache_swa_w512, cache_swa_w2048), 'S': state_hgrn_S}
    rs = _trunk(xs, mods_s, tables_s, p, w, st,
                nbatch=bs, chunk=SAMPLE_PAD, valid=ts, hg_block=SAMPLE_PAD, hg_sub=SAMPLE_PAD)

    def kv_rows(r, nb, t, keep_from, keep_to):
        ak, av = r['kv']
        out = []
        for g in range(len(SWA_GROUPS)):
            cols = slice(g * SWA_OUT, (g + 1) * SWA_OUT)
            k = ak.reshape(nb, t, SWA_WIDTH)[:, :, cols].reshape(nb, t, SWA_HEADS, SWA_HD)
            v = av.reshape(nb, t, SWA_WIDTH)[:, :, cols].reshape(nb, t, SWA_HEADS, SWA_HD)
            lo = keep_from(g)
            out.append(jnp.stack([k, v], axis=2)[:, lo:keep_to][None])
        return out

    rows_p = kv_rows(rp, bp, tp, lambda g: tp - min(SWA_GROUPS[g][0], tp), tp)
    rows_s = kv_rows(rs, bs, SAMPLE_PAD, lambda g: 0, ts)
    y_p = rp['y'].reshape(bp, tp, d)
    y_s = rs['y'].reshape(bs, SAMPLE_PAD, d)[:, :ts]
    return (y_p, y_s, rp['C'][None], rs['C'][None], rp['n'][None], rs['n'][None], rp['m'][None], rs['m'][None],
            rows_p[0], rows_s[0], rows_p[1], rows_s[1], rows_p[2], rows_s[2], rp['S'][None], rs['S'][None])
```

```python
import functools

import jax
import jax.numpy as jnp
import numpy as np
from jax import lax
from jax.experimental import pallas as pl
from jax.experimental.pallas import tpu as pltpu

f32 = jnp.float32
bf16 = jnp.bfloat16

D_MODEL = 1024
PAST_LEN = 8192
ML_HEADS = 4
ML_DK = 128
ML_WIDTH = 512
ML_CHUNK = 128
SWA_GROUPS = ((128, 1), (512, 4), (2048, 16))
SWA_N = 128
SWA_HEADS = 4
SWA_HD = 64
SWA_WIDTH = 768
SWA_OUT = SWA_HEADS * SWA_HD
ROT_DIM = 16
ROPE_THETA = 500000.0
HG_HEADS = 8
HG_SUB = 16
PEER_KEYS = 128
PEER_EXPERTS = PEER_KEYS * PEER_KEYS
PEER_HEADS = 8
PEER_TOPK = 16
EPS = 1e-6
LANE = 128
TOK_CHUNK = 256
SAMPLE_PAD = 8
NEG = -1e30

EV_GATE = 18
EV_M = 5
EV_COLS = 36 * LANE


def _peer_chunk(m_rows):
    return min(2 * TOK_CHUNK, m_rows)


def _params(sem, vmem_mb):
    return pltpu.CompilerParams(dimension_semantics=sem, vmem_limit_bytes=vmem_mb * 1024 * 1024)


def _mod_spec(mod, m_rows, tm, grid_rank):
    nb, r, d = mod.shape
    if r == 1:
        tiles_per = (m_rows // nb) // tm
        if grid_rank == 1:
            return pl.BlockSpec((None, 1, d), lambda i: (i // tiles_per, 0, 0))
        return pl.BlockSpec((None, 1, d), lambda i, j: (i // tiles_per, 0, 0))
    assert nb == 1 and r == m_rows
    if grid_rank == 1:
        return pl.BlockSpec((None, tm, d), lambda i: (0, i, 0))
    return pl.BlockSpec((None, tm, d), lambda i, j: (0, i, 0))


def _ada_kernel(c_ref, w_ref, b_ref, o_ref):
    c = c_ref[...]
    s = (c * jax.nn.sigmoid(c)).astype(bf16)
    o_ref[...] = jnp.dot(s, w_ref[...].astype(bf16), preferred_element_type=f32) + b_ref[...]


def ada_all(c_all, ada_w, ada_b):
    depth, d, n = ada_w.shape
    bp = c_all.shape[0]
    tn = 1536
    return pl.pallas_call(
        _ada_kernel,
        grid=(depth, n // tn),
        in_specs=[pl.BlockSpec((bp, d), lambda l, j: (0, 0)),
                  pl.BlockSpec((None, d, tn), lambda l, j: (l, 0, j)),
                  pl.BlockSpec((None, 1, tn), lambda l, j: (l, 0, j))],
        out_specs=pl.BlockSpec((None, bp, tn), lambda l, j: (l, 0, j)),
        out_shape=jax.ShapeDtypeStruct((depth, bp, n), f32),
        compiler_params=_params(("arbitrary", "arbitrary"), 40),
        name="ada_mod",
    )(c_all, ada_w, ada_b.reshape(depth, 1, n))


def _nmm_kernel(x_ref, g_ref, sc_ref, sh_ref, w_ref, o_ref, *ht_ref):
    x = x_ref[...]
    y = x * lax.rsqrt(jnp.mean(x * x, axis=-1, keepdims=True) + EPS) * g_ref[...]
    hb = (y * (1.0 + sc_ref[...]) + sh_ref[...]).astype(bf16)
    o_ref[...] = jnp.dot(hb, w_ref[...], preferred_element_type=f32)
    if ht_ref:
        ht_ref[0][...] = hb.T


def norm_mod_matmul(x, g, sc, sh, w, *, emit_ht=False):
    m, d = x.shape
    n = w.shape[1]
    tm = TOK_CHUNK
    out_shape = [jax.ShapeDtypeStruct((m, n), f32)]
    out_specs = [pl.BlockSpec((tm, n), lambda i: (i, 0))]
    if emit_ht:
        per = _peer_chunk(m) // tm
        out_shape.append(jax.ShapeDtypeStruct((m // (per * tm), d, per * tm), bf16))
        out_specs.append(pl.BlockSpec((None, d, tm), lambda i: (i // per, 0, i % per)))
    res = pl.pallas_call(
        _nmm_kernel,
        grid=(m // tm,),
        in_specs=[pl.BlockSpec((tm, d), lambda i: (i, 0)),
                  pl.BlockSpec((1, d), lambda i: (0, 0)),
                  _mod_spec(sc, m, tm, 1), _mod_spec(sh, m, tm, 1),
                  pl.BlockSpec((d, n), lambda i: (0, 0))],
        out_specs=out_specs,
        out_shape=out_shape,
        compiler_params=_params(("arbitrary",), 52),
        name="norm_mod_matmul",
    )(x, g.reshape(1, d), sc, sh, w)
    return res if emit_ht else res[0]


def _resid_kernel(n_y, x_ref, g_ref, *refs):
    ys, ws, o_ref = refs[:n_y], refs[n_y:2 * n_y], refs[2 * n_y]
    acc = None
    for y_ref, w_ref in zip(ys, ws):
        t = jnp.dot(y_ref[...].astype(bf16), w_ref[...], preferred_element_type=f32)
        acc = t if acc is None else acc + t
    o_ref[...] = x_ref[...] + g_ref[...] * acc


def resid_matmul(x, gate, ys, ws):
    m, d = x.shape
    tm = TOK_CHUNK
    n_y = len(ys)
    in_specs = [pl.BlockSpec((tm, d), lambda i: (i, 0)), _mod_spec(gate, m, tm, 1)]
    in_specs += [pl.BlockSpec((tm, y.shape[1]), lambda i: (i, 0)) for y in ys]
    in_specs += [pl.BlockSpec(w.shape, lambda i: (0, 0)) for w in ws]
    return pl.pallas_call(
        functools.partial(_resid_kernel, n_y),
        grid=(m // tm,),
        in_specs=in_specs,
        out_specs=pl.BlockSpec((tm, d), lambda i: (i, 0)),
        out_shape=jax.ShapeDtypeStruct((m, d), f32),
        compiler_params=_params(("arbitrary",), 40),
        name="resid_matmul",
    )(x, gate, *ys, *ws)


def _mlstm_kernel(valid, q_ref, k_ref, v_ref, o_ref, gt_ref, bif_ref, gain_ref, c0_ref, n0_ref, m0_ref,
                  hm_ref, c_ref, n_ref, m_ref):
    ch = pl.program_id(1)
    length = q_ref.shape[0]

    @pl.when(ch == 0)
    def _():
        c_ref[...] = c0_ref[...]
        n_ref[...] = n0_ref[...]
        m_ref[...] = m0_ref[...]

    gates = gt_ref[...] + bif_ref[...]
    lane = lax.broadcasted_iota(jnp.int32, gates.shape, 1)
    row = lax.broadcasted_iota(jnp.int32, (length, length), 0)
    col = lax.broadcasted_iota(jnp.int32, (length, length), 1)
    causal = row >= col
    tril = jnp.where(causal, 1.0, 0.0)
    tok = lax.broadcasted_iota(jnp.int32, (length, 1), 0)
    for h in range(ML_HEADS):
        ln = slice(h * LANE, (h + 1) * LANE)
        q = q_ref[:, ln]
        k = k_ref[:, ln] * (ML_DK ** -0.5)
        v = v_ref[:, ln]
        ig = jnp.sum(jnp.where(lane == h, gates, 0.0), axis=1, keepdims=True)
        fg = jnp.sum(jnp.where(lane == h + ML_HEADS, gates, 0.0), axis=1, keepdims=True)
        lf = jax.nn.log_sigmoid(fg)
        if valid < length:
            ig = jnp.where(tok < valid, ig, NEG)
            lf = jnp.where(tok < valid, lf, 0.0)
        fb = jnp.dot(tril, jnp.broadcast_to(lf, (length, LANE)), preferred_element_type=f32,
                     precision=lax.Precision.HIGHEST)
        fcol = fb[:, :1]
        fsq = fb[:, :length]
        rowv = jnp.sum(jnp.where(row == col, fsq - ig, 0.0), axis=0, keepdims=True)
        log_d = jnp.where(causal, fsq - rowv, -jnp.inf)
        m_prev = m_ref[h][:, :1]
        inter = fcol + m_prev
        m_loc = jnp.maximum(inter, jnp.max(log_d, axis=1, keepdims=True))
        w_intra = jnp.exp(log_d - m_loc)
        w_inter = jnp.exp(inter - m_loc)
        qb = q.astype(bf16)
        kb = k.astype(bf16)
        vb = v.astype(bf16)
        s = lax.dot_general(qb, kb, (((1,), (1,)), ((), ())), preferred_element_type=f32) * w_intra
        c_prev = c_ref[h]
        n_prev = n_ref[h]
        num = (jnp.dot(s.astype(bf16), vb, preferred_element_type=f32)
               + w_inter * jnp.dot(qb, c_prev.astype(bf16), preferred_element_type=f32))
        den = jnp.sum(s, axis=1, keepdims=True) + w_inter * jnp.sum(q * n_prev, axis=1, keepdims=True)
        den = jnp.maximum(jnp.abs(den), jnp.exp(-m_loc))
        hh = num / den
        hn = hh * lax.rsqrt(jnp.mean(hh * hh, axis=1, keepdims=True) + EPS) * gain_ref[:, ln]
        hm_ref[:, ln] = hn * jax.nn.sigmoid(o_ref[:, ln])
        f_last = fcol[length - 1:length, :]
        logw = f_last - fcol + ig
        m_new = jnp.maximum(f_last + m_prev, jnp.max(logw, axis=0, keepdims=True))
        wk = jnp.exp(logw - m_new)
        decay = jnp.exp(f_last + m_prev - m_new)
        kw = k * wk
        c_ref[h] = decay * c_prev + lax.dot_general(kw.astype(bf16), vb, (((0,), (0,)), ((), ())),
                                                    preferred_element_type=f32)
        n_ref[h] = decay * n_prev + jnp.sum(kw, axis=0, keepdims=True)
        m_ref[h] = jnp.broadcast_to(m_new, (1, LANE))


def mlstm(proj, b_if, ml_gain, c0, n0, m0, *, nbatch, chunk, valid):
    m = proj.shape[0]
    nc = (m // nbatch) // chunk
    bif = jnp.zeros((1, LANE), f32).at[0, :2 * ML_HEADS].set(b_if.astype(f32))
    tok = lambda off: pl.BlockSpec((chunk, ML_WIDTH), lambda b, c: (b * nc + c, off))
    st_c = pl.BlockSpec((None, ML_HEADS, ML_DK, LANE), lambda b, c: (b, 0, 0, 0))
    st_v = pl.BlockSpec((None, ML_HEADS, 1, LANE), lambda b, c: (b, 0, 0, 0))
    return pl.pallas_call(
        functools.partial(_mlstm_kernel, valid),
        grid=(nbatch, nc),
        in_specs=[tok(EV_M), tok(EV_M + 1), tok(EV_M + 2), tok(EV_M + 3),
                  pl.BlockSpec((chunk, LANE), lambda b, c: (b * nc + c, EV_GATE)),
                  pl.BlockSpec((1, LANE), lambda b, c: (0, 0)),
                  pl.BlockSpec((1, ML_WIDTH), lambda b, c: (0, 0)),
                  st_c, st_v, st_v],
        out_specs=[pl.BlockSpec((chunk, ML_WIDTH), lambda b, c: (b * nc + c, 0)), st_c, st_v, st_v],
        out_shape=[jax.ShapeDtypeStruct((m, ML_WIDTH), f32),
                   jax.ShapeDtypeStruct(c0.shape, f32), jax.ShapeDtypeStruct(n0.shape, f32),
                   jax.ShapeDtypeStruct(m0.shape, f32)],
        compiler_params=_params(("arbitrary", "arbitrary"), 32),
        name="mlstm",
    )(proj, proj, proj, proj, proj, bif, ml_gain.reshape(1, ML_WIDTH), c0, n0, m0)


def _qkprep_kernel(q_ref, k_ref, v_ref, qg_ref, kg_ref, ta_ref, tb_ref, tc_ref, bd_ref, qo_ref, ko_ref, vo_ref):
    ta, tb, tc, bd = ta_ref[...], tb_ref[...], tc_ref[...], bd_ref[...]
    for src, dst, gain_ref in ((q_ref, qo_ref, qg_ref), (k_ref, ko_ref, kg_ref)):
        gain = gain_ref[...]
        for blk in range(SWA_WIDTH // LANE):
            x = src[:, blk * LANE:(blk + 1) * LANE]
            ms = jnp.dot(x * x, bd, preferred_element_type=f32, precision=lax.Precision.HIGHEST)
            xn = x * lax.rsqrt(ms + EPS) * gain
            rot = xn * ta + pltpu.roll(xn, LANE - ROT_DIM // 2, 1) * tb + pltpu.roll(xn, ROT_DIM // 2, 1) * tc
            dst[:, blk * LANE:(blk + 1) * LANE] = rot
    vo_ref[...] = v_ref[...]


def _rotary_tables(pos):
    half = ROT_DIM // 2
    inv = ROPE_THETA ** (-jnp.arange(half, dtype=f32) / half)
    ang = pos.astype(f32)[:, None] * inv
    cos, sin = jnp.cos(ang), jnp.sin(ang)
    t = pos.shape[0]
    one = jnp.ones((t, SWA_HD - ROT_DIM), f32)
    zero = jnp.zeros((t, SWA_HD - ROT_DIM), f32)
    zh = jnp.zeros((t, half), f32)
    a = jnp.concatenate([cos, cos, one], axis=1)
    b = jnp.concatenate([-sin, zh, zero], axis=1)
    c = jnp.concatenate([zh, sin, zero], axis=1)
    return tuple(jnp.concatenate([u, u], axis=1) for u in (a, b, c))


def qk_prep(proj, qn_g, kn_g, tables):
    m = proj.shape[0]
    tm = TOK_CHUNK
    ttab = tables[0].shape[0]
    ntab = ttab // tm
    bd = np.kron(np.eye(LANE // SWA_HD), np.full((SWA_HD, SWA_HD), 1.0 / SWA_HD)).astype(np.float32)
    gain = lambda g: jnp.tile(g.astype(f32), LANE // SWA_HD).reshape(1, LANE)
    col = lambda blk: pl.BlockSpec((tm, SWA_WIDTH), lambda i: (i, blk))
    tab = pl.BlockSpec((tm, LANE), lambda i: (i % ntab, 0))
    one = pl.BlockSpec((1, LANE), lambda i: (0, 0))
    out = pl.BlockSpec((tm, SWA_WIDTH), lambda i: (i, 0))
    return pl.pallas_call(
        _qkprep_kernel,
        grid=(m // tm,),
        in_specs=[col(0), col(1), col(2), one, one, tab, tab, tab, pl.BlockSpec((LANE, LANE), lambda i: (0, 0))],
        out_specs=[out, out, out],
        out_shape=[jax.ShapeDtypeStruct((m, SWA_WIDTH), f32)] * 3,
        compiler_params=_params(("arbitrary",), 32),
        name="qk_prep",
    )(proj, proj, proj, gain(qn_g), gain(kn_g), *tables, jnp.asarray(bd))


def _swa_kernel(q_ref, kc_ref, kp_ref, vc_ref, vp_ref, o_ref, l_ref):
    c = pl.program_id(3)
    n = SWA_N
    q = q_ref[...]
    kk = jnp.concatenate([kp_ref[...], kc_ref[...]], axis=0).astype(bf16)
    vv = jnp.concatenate([vp_ref[...], vc_ref[...]], axis=0).astype(bf16)
    a = lax.broadcasted_iota(jnp.int32, (n, 2 * n), 0)
    ko = lax.broadcasted_iota(jnp.int32, (n, 2 * n), 1)
    mask = (ko >= a) & (ko <= a + n) & ((c > 0) | (ko >= n))
    lane = lax.broadcasted_iota(jnp.int32, (n, LANE), 1)
    out = jnp.zeros((n, LANE), f32)
    lse = jnp.zeros((n, LANE), f32)
    for hh in range(LANE // SWA_HD):
        head = (lane >= hh * SWA_HD) & (lane < (hh + 1) * SWA_HD)
        qh = jnp.where(head, q, 0.0).astype(bf16)
        s = lax.dot_general(qh, kk, (((1,), (1,)), ((), ())), preferred_element_type=f32) * (SWA_HD ** -0.5)
        s = jnp.where(mask, s, -jnp.inf)
        mx = jnp.max(s, axis=1, keepdims=True)
        p = jnp.exp(s - mx)
        den = jnp.sum(p, axis=1, keepdims=True)
        o = jnp.dot(p.astype(bf16), vv, preferred_element_type=f32) / den
        out = jnp.where(head, o, out)
        lse = jnp.where(head, mx + jnp.log(den), lse)
    o_ref[...] = out
    l_ref[...] = lse


def swa_prompt(aq, ak, av, group, *, nbatch):
    m = aq.shape[0]
    win, dil = SWA_GROUPS[group]
    n = SWA_N
    t = m // nbatch
    nb = t // (dil * n)
    rows = m // dil
    nblk = SWA_WIDTH // LANE
    view = lambda x: x.reshape(rows, dil * SWA_WIDTH)
    cur = pl.BlockSpec((n, LANE), lambda b, r, p, c: (b * nb + c, r * nblk + group * 2 + p))
    prev = pl.BlockSpec((n, LANE), lambda b, r, p, c: (b * nb + jnp.maximum(c - 1, 0), r * nblk + group * 2 + p))
    outs = pl.BlockSpec((n, LANE), lambda b, r, p, c: (b * nb + c, r * 2 + p))
    o, l = pl.pallas_call(
        _swa_kernel,
        grid=(nbatch, dil, 2, nb),
        in_specs=[cur, cur, prev, cur, prev],
        out_specs=[outs, outs],
        out_shape=[jax.ShapeDtypeStruct((rows, dil * SWA_OUT), f32)] * 2,
        compiler_params=_params(("arbitrary",) * 4, 32),
        name=f"swa_prompt_w{win}",
    )(view(aq), view(ak), view(ak), view(av), view(av))
    return o.reshape(m, SWA_OUT), l.reshape(m, SWA_OUT)


def _merge_kernel(o0, o1, o2, l0, l1, l2, out_ref):
    ls = (l0[...], l1[...], l2[...])
    mx = jnp.maximum(jnp.maximum(ls[0], ls[1]), ls[2])
    es = [jnp.exp(l - mx) for l in ls]
    tot = es[0] + es[1] + es[2]
    out_ref[...] = (es[0] * o0[...] + es[1] * o1[...] + es[2] * o2[...]) / tot


def swa_merge(outs, lses):
    m = outs[0].shape[0]
    tm = 512
    spec = pl.BlockSpec((tm, SWA_OUT), lambda i: (i, 0))
    return pl.pallas_call(
        _merge_kernel, grid=(m // tm,), in_specs=[spec] * 6, out_specs=spec,
        out_shape=jax.ShapeDtypeStruct((m, SWA_OUT), f32),
        compiler_params=_params(("arbitrary",), 32), name="swa_merge",
    )(*outs, *lses)


def _swa_sample_kernel(nvalid, q_ref, k_ref, v_ref, b0_ref, b1_ref, b2_ref, bd_ref, o_ref):
    bd = bd_ref[...]
    n = SWA_N
    bufs = (b0_ref, b1_ref, b2_ref)
    arow = lax.broadcasted_iota(jnp.int32, (n, SWA_OUT), 0)
    nrow = lax.broadcasted_iota(jnp.int32, (SAMPLE_PAD, SWA_OUT), 0)
    rows = []
    for l in range(nvalid):
        outs, lses = [], []
        for g, (win, dil) in enumerate(SWA_GROUPS):
            cols = slice(g * SWA_OUT, (g + 1) * SWA_OUT)
            ql = q_ref[l:l + 1, cols]
            kn, vn = k_ref[:, cols], v_ref[:, cols]
            r = l % dil
            kb = bufs[g][:, r * 2 * SWA_OUT:r * 2 * SWA_OUT + SWA_OUT]
            vb = bufs[g][:, r * 2 * SWA_OUT + SWA_OUT:(r + 1) * 2 * SWA_OUT]
            scale = SWA_HD ** -0.5
            sb = jnp.dot(kb * ql, bd, preferred_element_type=f32, precision=lax.Precision.HIGHEST) * scale
            sn = jnp.dot(kn * ql, bd, preferred_element_type=f32, precision=lax.Precision.HIGHEST) * scale
            sb = jnp.where(arow * dil + r >= l, sb, -jnp.inf)
            ok_new = (nrow <= l) & (nrow < nvalid) & ((l - nrow) % dil == 0)
            sn = jnp.where(ok_new, sn, -jnp.inf)
            mx = jnp.maximum(jnp.max(sb, axis=0, keepdims=True), jnp.max(sn, axis=0, keepdims=True))
            pb, pn = jnp.exp(sb - mx), jnp.exp(sn - mx)
            den = jnp.sum(pb, axis=0, keepdims=True) + jnp.sum(pn, axis=0, keepdims=True)
            o = (jnp.sum(pb * vb, axis=0, keepdims=True) + jnp.sum(pn * vn, axis=0, keepdims=True)) / den
            outs.append(o)
            lses.append(mx + jnp.log(den))
        mx = jnp.maximum(jnp.maximum(lses[0], lses[1]), lses[2])
        es = [jnp.exp(x - mx) for x in lses]
        rows.append((es[0] * outs[0] + es[1] * outs[1] + es[2] * outs[2]) / (es[0] + es[1] + es[2]))
    rows.append(jnp.zeros((SAMPLE_PAD - nvalid, SWA_OUT), f32))
    o_ref[...] = jnp.concatenate(rows, axis=0)


def swa_sample(aq, ak, av, caches, *, nbatch, nvalid):
    m = aq.shape[0]
    bd = np.kron(np.eye(SWA_HEADS), np.ones((SWA_HD, SWA_HD))).astype(np.float32)
    tok = pl.BlockSpec((SAMPLE_PAD, SWA_WIDTH), lambda b: (b, 0))
    views, specs = [], []
    for (win, dil), buf in zip(SWA_GROUPS, caches):
        assert buf.shape[1] == win and nvalid <= dil * 4 and nvalid <= SWA_N
        views.append(buf.reshape(nbatch, win // dil, dil * 2 * SWA_OUT))
        width = min(dil, nvalid) * 2 * SWA_OUT
        specs.append(pl.BlockSpec((None, SWA_N, width), lambda b: (b, 0, 0)))
    return pl.pallas_call(
        functools.partial(_swa_sample_kernel, nvalid),
        grid=(nbatch,),
        in_specs=[tok, tok, tok, *specs, pl.BlockSpec((SWA_OUT, SWA_OUT), lambda b: (0, 0))],
        out_specs=pl.BlockSpec((SAMPLE_PAD, SWA_OUT), lambda b: (b, 0)),
        out_shape=jax.ShapeDtypeStruct((m, SWA_OUT), f32),
        compiler_params=_params(("arbitrary",), 32),
        name="swa_sample",
    )(aq, ak, av, *views, jnp.asarray(bd))


def _hgrn_kernel(layer, sub, valid, q_ref, f_ref, i_ref, g_ref, gam_ref, gain_ref, s0_ref, o_ref, s_ref, st_ref):
    blk = pl.program_id(1)
    nblk = pl.num_programs(1)
    length = q_ref.shape[0]

    @pl.when(blk == 0)
    def _():
        for h in range(HG_HEADS):
            st_ref[h] = s0_ref[h].T

    sm = jax.nn.softmax(gam_ref[...], axis=0)
    run = sm[0:1]
    for kk in range(1, layer + 1):
        run = run + sm[kk:kk + 1]
    lb = run - sm[0:1]
    log_lb, log_1m = jnp.log(lb), jnp.log1p(-lb)
    gain = gain_ref[...]
    r = lax.broadcasted_iota(jnp.int32, (sub, sub), 0)
    c = lax.broadcasted_iota(jnp.int32, (sub, sub), 1)
    tril = jnp.where(r >= c, 1.0, 0.0)
    bi = lax.broadcasted_iota(jnp.int32, (HG_HEADS * sub, HG_HEADS * sub * sub), 0)
    bf = lax.broadcasted_iota(jnp.int32, (HG_HEADS * sub, HG_HEADS * sub * sub), 1)
    bsel = jnp.where((bf // sub == bi) & (bf % sub <= bi % sub), 1.0, 0.0).astype(bf16)
    ones = jnp.ones((LANE, LANE), bf16)

    def step(sc, carry):
        r0 = pl.multiple_of(sc * sub, sub)
        rows = pl.ds(r0, sub)
        q, fr, v, gg = q_ref[rows, :], f_ref[rows, :], i_ref[rows, :], g_ref[rows, :]
        lg = jnp.logaddexp(log_lb, log_1m + jax.nn.log_sigmoid(fr))
        kin = (1.0 - lb) * jax.nn.sigmoid(-fr)
        if valid < length:
            tok = lax.broadcasted_iota(jnp.int32, (sub, 1), 0) + r0
            lg = jnp.where(tok < valid, lg, 0.0)
            kin = jnp.where(tok < valid, kin, 0.0)
        gc = jnp.dot(tril, lg, preferred_element_type=f32, precision=lax.Precision.HIGHEST)
        ge = gc[sub - 1:sub, :]
        qd = (q * jnp.exp(gc)).astype(bf16)
        kd = (kin * jnp.exp(ge - gc)).astype(bf16)
        dge = jnp.exp(ge)
        gate = gg * jax.nn.sigmoid(gg)
        prods, vts = [], []
        for h in range(HG_HEADS):
            ln = slice(h * LANE, (h + 1) * LANE)
            gch, kh, qh = gc[:, ln], kin[:, ln], q[:, ln]
            prods += [jnp.exp(jnp.minimum(gch[i:i + 1, :] - gch, 0.0)) * kh * qh[i:i + 1, :] for i in range(sub)]
            vts += [v[:, ln]] * sub
        red = jnp.dot(jnp.concatenate(prods, axis=0).astype(bf16), ones, preferred_element_type=f32)
        o_intra = jnp.dot(bsel, (red * jnp.concatenate(vts, axis=0)).astype(bf16), preferred_element_type=f32)
        for h in range(HG_HEADS):
            ln = slice(h * LANE, (h + 1) * LANE)
            st = st_ref[h]
            o = o_intra[h * sub:(h + 1) * sub, :] + lax.dot_general(qd[:, ln], st.astype(bf16), (((1,), (1,)), ((), ())),
                                                                    preferred_element_type=f32)
            st_ref[h] = st * dge[:, ln] + lax.dot_general(v[:, ln].astype(bf16), kd[:, ln], (((0,), (0,)), ((), ())),
                                                          preferred_element_type=f32)
            on = o * lax.rsqrt(jnp.mean(o * o, axis=1, keepdims=True) + EPS) * gain[:, ln]
            o_ref[rows, ln] = on * gate[:, ln]
        return carry

    lax.fori_loop(0, length // sub, step, 0)

    @pl.when(blk == nblk - 1)
    def _():
        for h in range(HG_HEADS):
            s_ref[h] = st_ref[h].T


def hgrn2(proj, lb_gamma, og_gain, s0, *, layer, nbatch, block, sub, valid):
    m = proj.shape[0]
    nblk = (m // nbatch) // block
    depth = lb_gamma.shape[0]
    tok = lambda off: pl.BlockSpec((block, D_MODEL), lambda b, c: (b * nblk + c, off))
    st = pl.BlockSpec((None, HG_HEADS, LANE, LANE), lambda b, c: (b, 0, 0, 0))
    return pl.pallas_call(
        functools.partial(_hgrn_kernel, layer, sub, valid),
        grid=(nbatch, nblk),
        in_specs=[tok(0), tok(1), tok(2), tok(3),
                  pl.BlockSpec((depth, D_MODEL), lambda b, c: (0, 0)),
                  pl.BlockSpec((1, D_MODEL), lambda b, c: (0, 0)), st],
        out_specs=[pl.BlockSpec((block, D_MODEL), lambda b, c: (b * nblk + c, 0)), st],
        out_shape=[jax.ShapeDtypeStruct((m, D_MODEL), f32), jax.ShapeDtypeStruct(s0.shape, f32)],
        scratch_shapes=[pltpu.VMEM((HG_HEADS, LANE, LANE), f32)],
        compiler_params=_params(("arbitrary",) * 2, 40),
        name="hgrn2",
    )(proj, proj, proj, proj, lb_gamma, og_gain.reshape(1, D_MODEL), s0)


def _cand_pairs():
    return [(i, j) for i in range(PEER_TOPK) for j in range(PEER_TOPK) if (i + 1) * (j + 1) <= PEER_TOPK]


def _top_rows(x, count, with_rank=False):
    rows, cur = [], x
    rank = jnp.full(x.shape, float(count), f32) if with_rank else None
    for r in range(count):
        mx = jnp.max(cur, axis=0, keepdims=True)
        rows.append(mx)
        hit = cur == mx
        if with_rank:
            rank = jnp.where(hit, float(r), rank)
        cur = jnp.where(hit, -jnp.inf, cur)
    return rows, rank


def _route_kernel(q_ref, keys_ref, rk_ref, ki_ref, wa_ref, eb_ref):
    pairs = _cand_pairs()
    pad = (-len(pairs)) % 8
    for h in range(PEER_HEADS):
        qh = q_ref[:, h * LANE:(h + 1) * LANE].astype(bf16)
        nt = (((1,), (1,)), ((), ()))
        a = lax.dot_general(keys_ref[2 * h], qh, nt, preferred_element_type=f32)
        b = lax.dot_general(keys_ref[2 * h + 1], qh, nt, preferred_element_type=f32)
        ta, _ = _top_rows(a, PEER_TOPK)
        tb, rank_b = _top_rows(b, PEER_TOPK, with_rank=True)
        cand = jnp.concatenate([ta[i] + tb[j] for i, j in pairs]
                               + [jnp.full((pad, a.shape[1]), -jnp.inf, f32)], axis=0)
        tau = _top_rows(cand, PEER_TOPK)[0][-1]
        top = ta[0] + tb[0]
        z = jnp.sum(jnp.where(cand >= tau, jnp.exp(cand - top), 0.0), axis=0, keepdims=True)
        thr = tau - a
        count = jnp.zeros_like(a)
        for r in range(PEER_TOPK):
            count = count + jnp.where(tb[r] >= thr, 1.0, 0.0)
        rows = slice(h * PEER_KEYS, (h + 1) * PEER_KEYS)
        rk_ref[rows, :] = rank_b.astype(bf16)
        ki_ref[rows, :] = count
        wa_ref[rows, :] = jnp.exp(a - ta[0]) / z
        eb_ref[rows, :] = jnp.exp(b - tb[0]).astype(bf16)


def peer_route(q, keys_pad):
    m = q.shape[0]
    tm = TOK_CHUNK
    per = _peer_chunk(m) // tm
    out = pl.BlockSpec((None, PEER_HEADS * PEER_KEYS, tm), lambda i: (i // per, 0, i % per))
    shape = lambda dt: jax.ShapeDtypeStruct((m // (per * tm), PEER_HEADS * PEER_KEYS, per * tm), dt)
    return pl.pallas_call(
        _route_kernel,
        grid=(m // tm,),
        in_specs=[pl.BlockSpec((tm, D_MODEL), lambda i: (i, 0)),
                  pl.BlockSpec(keys_pad.shape, lambda i: (0, 0, 0))],
        out_specs=[out] * 4,
        out_shape=[shape(bf16), shape(f32), shape(f32), shape(bf16)],
        compiler_params=_params(("arbitrary",), 40),
        name="peer_route",
    )(q, keys_pad)


def _gelu(x):
    return 0.5 * x * (1.0 + lax.erf(x * (2.0 ** -0.5)))


def _table_prep_kernel(u_ref, v_ref, uo_ref, vo_ref):
    uo_ref[...] = u_ref[...].astype(bf16)
    vo_ref[...] = v_ref[...].astype(bf16).T


def peer_tables(u, v):
    depth, e, d = u.shape
    te = 512
    return pl.pallas_call(
        _table_prep_kernel,
        grid=(depth, e // te),
        in_specs=[pl.BlockSpec((None, te, d), lambda l, j: (l, j, 0))] * 2,
        out_specs=[pl.BlockSpec((None, te, d), lambda l, j: (l, j, 0)),
                   pl.BlockSpec((None, d, te), lambda l, j: (l, 0, j))],
        out_shape=[jax.ShapeDtypeStruct((depth, e, d), bf16), jax.ShapeDtypeStruct((depth, d, e), bf16)],
        compiler_params=_params(("arbitrary", "arbitrary"), 32),
        name="peer_tables",
    )(u, v)


def _peer_dense_kernel(ht_ref, u_ref, vt_ref, rk_ref, ki_ref, wa_ref, eb_ref, x_ref, g_ref, o_ref,
                       acc_ref, act_ref, coef_ref):
    j = pl.program_id(1)
    nj = pl.num_programs(1)
    te, tm = act_ref.shape
    ni = te // PEER_KEYS
    width = 8 * 16 * LANE // tm

    @pl.when(j == 0)
    def _():
        acc_ref[...] = jnp.zeros_like(acc_ref)

    act_ref[...] = _gelu(jnp.dot(u_ref[...], ht_ref[...], preferred_element_type=f32)).astype(bf16)
    for ii in range(ni):
        i = j * ni + ii
        counts = [ki_ref[pl.ds(h * PEER_KEYS + i, 1), :].astype(bf16) for h in range(PEER_HEADS)]
        was = [wa_ref[pl.ds(h * PEER_KEYS + i, 1), :].astype(bf16) for h in range(PEER_HEADS)]
        for part in range(PEER_KEYS // width):
            gate = jnp.zeros((width, tm), bf16)
            for h in range(PEER_HEADS):
                rows = slice(h * PEER_KEYS + part * width, h * PEER_KEYS + (part + 1) * width)
                gate = gate + jnp.where(rk_ref[rows, :] < counts[h], eb_ref[rows, :], 0) * was[h]
            blk = slice(ii * PEER_KEYS + part * width, ii * PEER_KEYS + (part + 1) * width)
            coef_ref[blk, :] = gate * act_ref[blk, :]
    acc_ref[...] += jnp.dot(vt_ref[...], coef_ref[...], preferred_element_type=f32)

    @pl.when(j == nj - 1)
    def _():
        o_ref[...] = x_ref[...] + g_ref[...] * acc_ref[...].T


def peer_dense(ht, u, vt, layer, route, x, gate):
    m, d = x.shape
    tm = _peer_chunk(m)
    te = 1024
    rt = pl.BlockSpec((None, PEER_HEADS * PEER_KEYS, tm), lambda i, j: (i, 0, 0))
    assert ht.shape == (m // tm, d, tm)
    return pl.pallas_call(
        _peer_dense_kernel,
        grid=(m // tm, PEER_EXPERTS // te),
        in_specs=[pl.BlockSpec((None, d, tm), lambda i, j: (i, 0, 0)),
                  pl.BlockSpec((None, te, d), lambda i, j: (layer, j, 0)),
                  pl.BlockSpec((None, d, te), lambda i, j: (layer, 0, j)),
                  rt, rt, rt, rt,
                  pl.BlockSpec((tm, d), lambda i, j: (i, 0)),
                  _mod_spec(gate, m, tm, 2)],
        out_specs=pl.BlockSpec((tm, d), lambda i, j: (i, 0)),
        out_shape=jax.ShapeDtypeStruct((m, d), f32),
        scratch_shapes=[pltpu.VMEM((d, tm), f32), pltpu.VMEM((te, tm), bf16), pltpu.VMEM((te, tm), bf16)],
        compiler_params=_params(("arbitrary", "arbitrary"), 56),
        name="peer_dense",
    )(ht, u, vt, *route, x, gate)


def peer_layer(x, g, sc, sh, gate, w_q, keys_pad, u, vt, layer):
    q, ht = norm_mod_matmul(x, g, sc, sh, w_q, emit_ht=True)
    route = peer_route(q, keys_pad)
    return peer_dense(ht, u, vt, layer, route, x, gate)


def _prep_weights(p):
    w = {}
    ev = p['even_w_in'][0]
    cuts = np.cumsum([ML_WIDTH] * 4 + [2 * ML_HEADS] + [SWA_WIDTH] * 2)
    mq_mo, mg, att = ev[:, :cuts[3]], ev[:, cuts[3]:cuts[4]], ev[:, cuts[4]:]
    pad = jnp.zeros((D_MODEL, 2 * LANE - 2 * ML_HEADS), ev.dtype)
    w['even_in'] = jnp.concatenate([att, mg, pad, mq_mo], axis=1).astype(bf16)
    wo = p['even_w_out'][0].astype(bf16)
    w['even_out'] = (wo[:ML_WIDTH], wo[ML_WIDTH:])
    w['odd_in'] = p['odd_w_in'][0].astype(bf16)
    w['odd_out'] = p['odd_w_out'][0].astype(bf16)
    w['peer_q'] = p['peer_w_q'].astype(bf16)
    sk = p['peer_sub_keys'].astype(bf16)
    zero = jnp.zeros_like(sk)
    half0 = jnp.concatenate([sk[:, :, 0], zero[:, :, 0]], axis=-1)
    half1 = jnp.concatenate([zero[:, :, 1], sk[:, :, 1]], axis=-1)
    depth = sk.shape[0]
    w['peer_keys'] = jnp.stack([half0, half1], axis=2).reshape(depth, 2 * PEER_HEADS, PEER_KEYS, LANE)
    w['peer_u'], w['peer_vt'] = peer_tables(p['peer_u'], p['peer_v'])
    return w


def _trunk(x, mods, tables, p, w, st, *, nbatch, chunk, valid, hg_block, hg_sub):
    m = x.shape[0]
    res = {}
    sh1, sc1, g1, sh2, sc2, g2 = mods[0]
    proj = norm_mod_matmul(x, p['norm_g'][0, 0], sc1, sh1, w['even_in'])
    if st is None:
        c0 = jnp.zeros((nbatch, ML_HEADS, ML_DK, LANE), f32)
        n0 = jnp.zeros((nbatch, ML_HEADS, 1, LANE), f32)
        m0 = jnp.zeros((nbatch, ML_HEADS, 1, LANE), f32)
    else:
        c0 = st['C'][0].astype(f32)
        n0 = st['n'][0].astype(f32)[:, :, None, :]
        m0 = jnp.broadcast_to(st['m'][0].astype(f32)[:, :, None, None], (nbatch, ML_HEADS, 1, LANE))
    hm, c_new, n_new, m_new = mlstm(proj, p['even_b_if'][0], p['even_ml_gain'][0], c0, n0, m0,
                                    nbatch=nbatch, chunk=chunk, valid=valid)
    res['C'], res['n'], res['m'] = c_new, n_new[:, :, 0, :], m_new[:, :, 0, 0]
    aq, ak, av = qk_prep(proj, p['even_qn_g'][0], p['even_kn_g'][0], tables)
    if st is None:
        outs, lses = zip(*[swa_prompt(aq, ak, av, g, nbatch=nbatch) for g in range(len(SWA_GROUPS))])
        ha = swa_merge(outs, lses)
    else:
        ha = swa_sample(aq, ak, av, [b[0] for b in st['swa']], nbatch=nbatch, nvalid=valid)
    res['kv'] = (ak, av)
    x = resid_matmul(x, g1, [hm, ha], list(w['even_out']))
    x = peer_layer(x, p['norm_g'][0, 1], sc2, sh2, g2, w['peer_q'][0], w['peer_keys'][0], w['peer_u'], w['peer_vt'],0)
    sh1, sc1, g1, sh2, sc2, g2 = mods[1]
    proj = norm_mod_matmul(x, p['norm_g'][1, 0], sc1, sh1, w['odd_in'])
    s0 = jnp.zeros((nbatch, HG_HEADS, LANE, LANE), f32) if st is None else st['S'][0].astype(f32)
    ho, s_new = hgrn2(proj, p['odd_lb_gamma'], p['odd_og_gain'][0], s0, layer=1, nbatch=nbatch,
                      block=hg_block, sub=hg_sub, valid=valid)
    res['S'] = s_new
    x = resid_matmul(x, g1, [ho], [w['odd_out']])
    x = peer_layer(x, p['norm_g'][1, 1], sc2, sh2, g2, w['peer_q'][1], w['peer_keys'][1], w['peer_u'], w['peer_vt'],1)
    res['y'] = x
    return res


def kernel(x_prompt, x_sample, c_prompt, c_sample, state_mlstm_C, state_mlstm_n, state_mlstm_m, cache_swa_w128, cache_swa_w512, cache_swa_w2048, state_hgrn_S, norm_g, ada_w, ada_b, even_w_in, even_b_if, even_ml_gain, even_qn_g, even_kn_g, even_w_out, odd_w_in, odd_lb_gamma, odd_og_gain, odd_w_out, peer_w_q, peer_sub_keys, peer_u, peer_v):
    p = {'norm_g': norm_g, 'even_w_in': even_w_in, 'even_b_if': even_b_if, 'even_ml_gain': even_ml_gain,
         'even_qn_g': even_qn_g, 'even_kn_g': even_kn_g, 'even_w_out': even_w_out, 'odd_w_in': odd_w_in,
         'odd_lb_gamma': odd_lb_gamma, 'odd_og_gain': odd_og_gain, 'odd_w_out': odd_w_out, 'peer_w_q': peer_w_q,
         'peer_sub_keys': peer_sub_keys, 'peer_u': peer_u, 'peer_v': peer_v}
    w = _prep_weights(p)
    bp, tp, d = x_prompt.shape
    bs, ts, _ = x_sample.shape
    depth = ada_w.shape[0]
    assert tp % (SWA_GROUPS[-1][0]) == 0 and ts <= SAMPLE_PAD // 2 and (bs * SAMPLE_PAD) % TOK_CHUNK == 0

    nrows = -(-(bp + bs) // 8) * 8
    c_all = jnp.zeros((nrows, d), f32).at[:bp].set(c_prompt).at[bp:bp + bs].set(c_sample)
    mod = ada_all(c_all, ada_w, ada_b)
    split = lambda a: [a[..., k * d:(k + 1) * d] for k in range(6)]
    mods_p = [[u[:, None, :] for u in split(mod[l, :bp])] for l in range(depth)]
    mods_s = [[jnp.repeat(u, SAMPLE_PAD, axis=0)[None] for u in split(mod[l, bp:bp + bs])] for l in range(depth)]

    tables_p = _rotary_tables(jnp.arange(tp))
    rp = _trunk(x_prompt.reshape(bp * tp, d), mods_p, tables_p, p, w, None,
                nbatch=bp, chunk=ML_CHUNK, valid=ML_CHUNK, hg_block=ML_CHUNK, hg_sub=HG_SUB)

    xs = jnp.zeros((bs, SAMPLE_PAD, d), f32).at[:, :ts].set(x_sample).reshape(bs * SAMPLE_PAD, d)
    pos_s = jnp.tile(PAST_LEN + jnp.arange(SAMPLE_PAD), bs)
    tables_s = _rotary_tables(pos_s)
    st = {'C': state_mlstm_C, 'n': state_mlstm_n, 'm': state_mlstm_m,
          'swa': (cache_swa_w128, cache_swa_w512, cache_swa_w2048), 'S': state_hgrn_S}
    rs = _trunk(xs, mods_s, tables_s, p, w, st,
                nbatch=bs, chunk=SAMPLE_PAD, valid=ts, hg_block=SAMPLE_PAD, hg_sub=SAMPLE_PAD)

    def kv_rows(r, nb, t, keep_from, keep_to):
        ak, av = r['kv']
        out = []
        for g in range(len(SWA_GROUPS)):
            cols = slice(g * SWA_OUT, (g + 1) * SWA_OUT)
            k = ak.reshape(nb, t, SWA_WIDTH)[:, :, cols].reshape(nb, t, SWA_HEADS, SWA_HD)
            v = av.reshape(nb, t, SWA_WIDTH)[:, :, cols].reshape(nb, t, SWA_HEADS, SWA_HD)
            lo = keep_from(g)
            out.append(jnp.stack([k, v], axis=2)[:, lo:keep_to][None])
        return out

    rows_p = kv_rows(rp, bp, tp, lambda g: tp - min(SWA_GROUPS[g][0], tp), tp)
    rows_s = kv_rows(rs, bs, SAMPLE_PAD, lambda g: 0, ts)
    y_p = rp['y'].reshape(bp, tp, d)
    y_s = rs['y'].reshape(bs, SAMPLE_PAD, d)[:, :ts]
    return (y_p, y_s, rp['C'][None], rs['C'][None], rp['n'][None], rs['n'][None], rp['m'][None], rs['m'][None],
            rows_p[0], rows_s[0], rows_p[1], rows_s[1], rows_p[2], rows_s[2], rp['S'][None], rs['S'][None])
```

```python
import functools

import jax
import jax.numpy as jnp
import numpy as np
from jax import lax
from jax.experimental import pallas as pl
from jax.experimental.pallas import tpu as pltpu

f32 = jnp.float32
bf16 = jnp.bfloat16

D_MODEL = 1024
PAST_LEN = 8192
ML_HEADS = 4
ML_DK = 128
ML_WIDTH = 512
ML_CHUNK = 128
SWA_GROUPS = ((128, 1), (512, 4), (2048, 16))
SWA_N = 128
SWA_HEADS = 4
SWA_HD = 64
SWA_WIDTH = 768
SWA_OUT = SWA_HEADS * SWA_HD
ROT_DIM = 16
ROPE_THETA = 500000.0
HG_HEADS = 8
HG_SUB = 16
PEER_KEYS = 128
PEER_EXPERTS = PEER_KEYS * PEER_KEYS
PEER_HEADS = 8
PEER_TOPK = 16
EPS = 1e-6
LANE = 128
TOK_CHUNK = 256
SAMPLE_PAD = 8
NEG = -1e30

EV_GATE = 18
EV_M = 5
EV_COLS = 36 * LANE


def _peer_chunk(m_rows):
    return min(2 * TOK_CHUNK, m_rows)


def _params(sem, vmem_mb):
    return pltpu.CompilerParams(dimension_semantics=sem, vmem_limit_bytes=vmem_mb * 1024 * 1024)


def _mod_spec(mod, m_rows, tm, grid_rank):
    nb, r, d = mod.shape
    if r == 1:
        tiles_per = (m_rows // nb) // tm
        if grid_rank == 1:
            return pl.BlockSpec((None, 1, d), lambda i: (i // tiles_per, 0, 0))
        return pl.BlockSpec((None, 1, d), lambda i, j: (i // tiles_per, 0, 0))
    assert nb == 1 and r == m_rows
    if grid_rank == 1:
        return pl.BlockSpec((None, tm, d), lambda i: (0, i, 0))
    return pl.BlockSpec((None, tm, d), lambda i, j: (0, i, 0))


def _ada_kernel(c_ref, w_ref, b_ref, o_ref):
    c = c_ref[...]
    s = (c * jax.nn.sigmoid(c)).astype(bf16)
    o_ref[...] = jnp.dot(s, w_ref[...].astype(bf16), preferred_element_type=f32) + b_ref[...]


def ada_all(c_all, ada_w, ada_b):
    depth, d, n = ada_w.shape
    bp = c_all.shape[0]
    tn = 1536
    return pl.pallas_call(
        _ada_kernel,
        grid=(depth, n // tn),
        in_specs=[pl.BlockSpec((bp, d), lambda l, j: (0, 0)),
                  pl.BlockSpec((None, d, tn), lambda l, j: (l, 0, j)),
                  pl.BlockSpec((None, 1, tn), lambda l, j: (l, 0, j))],
        out_specs=pl.BlockSpec((None, bp, tn), lambda l, j: (l, 0, j)),
        out_shape=jax.ShapeDtypeStruct((depth, bp, n), f32),
        compiler_params=_params(("arbitrary", "arbitrary"), 40),
        name="ada_mod",
    )(c_all, ada_w, ada_b.reshape(depth, 1, n))


def _nmm_kernel(x_ref, g_ref, sc_ref, sh_ref, w_ref, o_ref, *ht_ref):
    x = x_ref[...]
    y = x * lax.rsqrt(jnp.mean(x * x, axis=-1, keepdims=True) + EPS) * g_ref[...]
    hb = (y * (1.0 + sc_ref[...]) + sh_ref[...]).astype(bf16)
    o_ref[...] = jnp.dot(hb, w_ref[...], preferred_element_type=f32)
    if ht_ref:
        ht_ref[0][...] = hb.T


def norm_mod_matmul(x, g, sc, sh, w, *, emit_ht=False):
    m, d = x.shape
    n = w.shape[1]
    tm = TOK_CHUNK
    out_shape = [jax.ShapeDtypeStruct((m, n), f32)]
    out_specs = [pl.BlockSpec((tm, n), lambda i: (i, 0))]
    if emit_ht:
        per = _peer_chunk(m) // tm
        out_shape.append(jax.ShapeDtypeStruct((m // (per * tm), d, per * tm), bf16))
        out_specs.append(pl.BlockSpec((None, d, tm), lambda i: (i // per, 0, i % per)))
    res = pl.pallas_call(
        _nmm_kernel,
        grid=(m // tm,),
        in_specs=[pl.BlockSpec((tm, d), lambda i: (i, 0)),
                  pl.BlockSpec((1, d), lambda i: (0, 0)),
                  _mod_spec(sc, m, tm, 1), _mod_spec(sh, m, tm, 1),
                  pl.BlockSpec((d, n), lambda i: (0, 0))],
        out_specs=out_specs,
        out_shape=out_shape,
        compiler_params=_params(("arbitrary",), 52),
        name="norm_mod_matmul",
    )(x, g.reshape(1, d), sc, sh, w)
    return res if emit_ht else res[0]


def _resid_kernel(n_y, x_ref, g_ref, *refs):
    ys, ws, o_ref = refs[:n_y], refs[n_y:2 * n_y], refs[2 * n_y]
    acc = None
    for y_ref, w_ref in zip(ys, ws):
        t = jnp.dot(y_ref[...].astype(bf16), w_ref[...], preferred_element_type=f32)
        acc = t if acc is None else acc + t
    o_ref[...] = x_ref[...] + g_ref[...] * acc


def resid_matmul(x, gate, ys, ws):
    m, d = x.shape
    tm = TOK_CHUNK
    n_y = len(ys)
    in_specs = [pl.BlockSpec((tm, d), lambda i: (i, 0)), _mod_spec(gate, m, tm, 1)]
    in_specs += [pl.BlockSpec((tm, y.shape[1]), lambda i: (i, 0)) for y in ys]
    in_specs += [pl.BlockSpec(w.shape, lambda i: (0, 0)) for w in ws]
    return pl.pallas_call(
        functools.partial(_resid_kernel, n_y),
        grid=(m // tm,),
        in_specs=in_specs,
        out_specs=pl.BlockSpec((tm, d), lambda i: (i, 0)),
        out_shape=jax.ShapeDtypeStruct((m, d), f32),
        compiler_params=_params(("arbitrary",), 40),
        name="resid_matmul",
    )(x, gate, *ys, *ws)


def _mlstm_kernel(valid, q_ref, k_ref, v_ref, o_ref, gt_ref, bif_ref, gain_ref, c0_ref, n0_ref, m0_ref,
                  hm_ref, c_ref, n_ref, m_ref):
    ch = pl.program_id(1)
    length = q_ref.shape[0]

    @pl.when(ch == 0)
    def _():
        c_ref[...] = c0_ref[...]
        n_ref[...] = n0_ref[...]
        m_ref[...] = m0_ref[...]

    gates = gt_ref[...] + bif_ref[...]
    lane = lax.broadcasted_iota(jnp.int32, gates.shape, 1)
    row = lax.broadcasted_iota(jnp.int32, (length, length), 0)
    col = lax.broadcasted_iota(jnp.int32, (length, length), 1)
    causal = row >= col
    tril = jnp.where(causal, 1.0, 0.0)
    tok = lax.broadcasted_iota(jnp.int32, (length, 1), 0)
    for h in range(ML_HEADS):
        ln = slice(h * LANE, (h + 1) * LANE)
        q = q_ref[:, ln]
        k = k_ref[:, ln] * (ML_DK ** -0.5)
        v = v_ref[:, ln]
        ig = jnp.sum(jnp.where(lane == h, gates, 0.0), axis=1, keepdims=True)
        fg = jnp.sum(jnp.where(lane == h + ML_HEADS, gates, 0.0), axis=1, keepdims=True)
        lf = jax.nn.log_sigmoid(fg)
        if valid < length:
            ig = jnp.where(tok < valid, ig, NEG)
            lf = jnp.where(tok < valid, lf, 0.0)
        fb = jnp.dot(tril, jnp.broadcast_to(lf, (length, LANE)), preferred_element_type=f32,
                     precision=lax.Precision.HIGHEST)
        fcol = fb[:, :1]
        fsq = fb[:, :length]
        rowv = jnp.sum(jnp.where(row == col, fsq - ig, 0.0), axis=0, keepdims=True)
        log_d = jnp.where(causal, fsq - rowv, -jnp.inf)
        m_prev = m_ref[h][:, :1]
        inter = fcol + m_prev
        m_loc = jnp.maximum(inter, jnp.max(log_d, axis=1, keepdims=True))
        w_intra = jnp.exp(log_d - m_loc)
        w_inter = jnp.exp(inter - m_loc)
        qb = q.astype(bf16)
        kb = k.astype(bf16)
        vb = v.astype(bf16)
        s = lax.dot_general(qb, kb, (((1,), (1,)), ((), ())), preferred_element_type=f32) * w_intra
        c_prev = c_ref[h]
        n_prev = n_ref[h]
        num = (jnp.dot(s.astype(bf16), vb, preferred_element_type=f32)
               + w_inter * jnp.dot(qb, c_prev.astype(bf16), preferred_element_type=f32))
        den = jnp.sum(s, axis=1, keepdims=True) + w_inter * jnp.sum(q * n_prev, axis=1, keepdims=True)
        den = jnp.maximum(jnp.abs(den), jnp.exp(-m_loc))
        hh = num / den
        hn = hh * lax.rsqrt(jnp.mean(hh * hh, axis=1, keepdims=True) + EPS) * gain_ref[:, ln]
        hm_ref[:, ln] = hn * jax.nn.sigmoid(o_ref[:, ln])
        f_last = fcol[length - 1:length, :]
        logw = f_last - fcol + ig
        m_new = jnp.maximum(f_last + m_prev, jnp.max(logw, axis=0, keepdims=True))
        wk = jnp.exp(logw - m_new)
        decay = jnp.exp(f_last + m_prev - m_new)
        kw = k * wk
        c_ref[h] = decay * c_prev + lax.dot_general(kw.astype(bf16), vb, (((0,), (0,)), ((), ())),
                                                    preferred_element_type=f32)
        n_ref[h] = decay * n_prev + jnp.sum(kw, axis=0, keepdims=True)
        m_ref[h] = jnp.broadcast_to(m_new, (1, LANE))


def mlstm(proj, b_if, ml_gain, c0, n0, m0, *, nbatch, chunk, valid):
    m = proj.shape[0]
    nc = (m // nbatch) // chunk
    bif = jnp.zeros((1, LANE), f32).at[0, :2 * ML_HEADS].set(b_if.astype(f32))
    tok = lambda off: pl.BlockSpec((chunk, ML_WIDTH), lambda b, c: (b * nc + c, off))
    st_c = pl.BlockSpec((None, ML_HEADS, ML_DK, LANE), lambda b, c: (b, 0, 0, 0))
    st_v = pl.BlockSpec((None, ML_HEADS, 1, LANE), lambda b, c: (b, 0, 0, 0))
    return pl.pallas_call(
        functools.partial(_mlstm_kernel, valid),
        grid=(nbatch, nc),
        in_specs=[tok(EV_M), tok(EV_M + 1), tok(EV_M + 2), tok(EV_M + 3),
                  pl.BlockSpec((chunk, LANE), lambda b, c: (b * nc + c, EV_GATE)),
                  pl.BlockSpec((1, LANE), lambda b, c: (0, 0)),
                  pl.BlockSpec((1, ML_WIDTH), lambda b, c: (0, 0)),
                  st_c, st_v, st_v],
        out_specs=[pl.BlockSpec((chunk, ML_WIDTH), lambda b, c: (b * nc + c, 0)), st_c, st_v, st_v],
        out_shape=[jax.ShapeDtypeStruct((m, ML_WIDTH), f32),
                   jax.ShapeDtypeStruct(c0.shape, f32), jax.ShapeDtypeStruct(n0.shape, f32),
                   jax.ShapeDtypeStruct(m0.shape, f32)],
        compiler_params=_params(("arbitrary", "arbitrary"), 32),
        name="mlstm",
    )(proj, proj, proj, proj, proj, bif, ml_gain.reshape(1, ML_WIDTH), c0, n0, m0)


def _qkprep_kernel(q_ref, k_ref, v_ref, qg_ref, kg_ref, ta_ref, tb_ref, tc_ref, bd_ref, qo_ref, ko_ref, vo_ref):
    ta, tb, tc, bd = ta_ref[...], tb_ref[...], tc_ref[...], bd_ref[...]
    for src, dst, gain_ref in ((q_ref, qo_ref, qg_ref), (k_ref, ko_ref, kg_ref)):
        gain = gain_ref[...]
        for blk in range(SWA_WIDTH // LANE):
            x = src[:, blk * LANE:(blk + 1) * LANE]
            ms = jnp.dot(x * x, bd, preferred_element_type=f32, precision=lax.Precision.HIGHEST)
            xn = x * lax.rsqrt(ms + EPS) * gain
            rot = xn * ta + pltpu.roll(xn, LANE - ROT_DIM // 2, 1) * tb + pltpu.roll(xn, ROT_DIM // 2, 1) * tc
            dst[:, blk * LANE:(blk + 1) * LANE] = rot
    vo_ref[...] = v_ref[...]


def _rotary_tables(pos):
    half = ROT_DIM // 2
    inv = ROPE_THETA ** (-jnp.arange(half, dtype=f32) / half)
    ang = pos.astype(f32)[:, None] * inv
    cos, sin = jnp.cos(ang), jnp.sin(ang)
    t = pos.shape[0]
    one = jnp.ones((t, SWA_HD - ROT_DIM), f32)
    zero = jnp.zeros((t, SWA_HD - ROT_DIM), f32)
    zh = jnp.zeros((t, half), f32)
    a = jnp.concatenate([cos, cos, one], axis=1)
    b = jnp.concatenate([-sin, zh, zero], axis=1)
    c = jnp.concatenate([zh, sin, zero], axis=1)
    return tuple(jnp.concatenate([u, u], axis=1) for u in (a, b, c))


def qk_prep(proj, qn_g, kn_g, tables):
    m = proj.shape[0]
    tm = TOK_CHUNK
    ttab = tables[0].shape[0]
    ntab = ttab // tm
    bd = np.kron(np.eye(LANE // SWA_HD), np.full((SWA_HD, SWA_HD), 1.0 / SWA_HD)).astype(np.float32)
    gain = lambda g: jnp.tile(g.astype(f32), LANE // SWA_HD).reshape(1, LANE)
    col = lambda blk: pl.BlockSpec((tm, SWA_WIDTH), lambda i: (i, blk))
    tab = pl.BlockSpec((tm, LANE), lambda i: (i % ntab, 0))
    one = pl.BlockSpec((1, LANE), lambda i: (0, 0))
    out = pl.BlockSpec((tm, SWA_WIDTH), lambda i: (i, 0))
    return pl.pallas_call(
        _qkprep_kernel,
        grid=(m // tm,),
        in_specs=[col(0), col(1), col(2), one, one, tab, tab, tab, pl.BlockSpec((LANE, LANE), lambda i: (0, 0))],
        out_specs=[out, out, out],
        out_shape=[jax.ShapeDtypeStruct((m, SWA_WIDTH), f32)] * 3,
        compiler_params=_params(("arbitrary",), 32),
        name="qk_prep",
    )(proj, proj, proj, gain(qn_g), gain(kn_g), *tables, jnp.asarray(bd))


def _swa_kernel(q_ref, kc_ref, kp_ref, vc_ref, vp_ref, o_ref, l_ref):
    c = pl.program_id(2)
    n = SWA_N
    q = q_ref[...]
    kk = jnp.concatenate([kp_ref[...], kc_ref[...]], axis=0).astype(bf16)
    vv = jnp.concatenate([vp_ref[...], vc_ref[...]], axis=0).astype(bf16)
    a = lax.broadcasted_iota(jnp.int32, (n, 2 * n), 0)
    ko = lax.broadcasted_iota(jnp.int32, (n, 2 * n), 1)
    mask = (ko >= a) & (ko <= a + n) & ((c > 0) | (ko >= n))
    lane = lax.broadcasted_iota(jnp.int32, (n, SWA_OUT), 1)
    out = jnp.zeros((n, SWA_OUT), f32)
    lse = jnp.zeros((n, SWA_OUT), f32)
    for hh in range(SWA_HEADS):
        head = (lane >= hh * SWA_HD) & (lane < (hh + 1) * SWA_HD)
        qh = jnp.where(head, q, 0.0).astype(bf16)
        s = lax.dot_general(qh, kk, (((1,), (1,)), ((), ())), preferred_element_type=f32) * (SWA_HD ** -0.5)
        s = jnp.where(mask, s, -jnp.inf)
        mx = jnp.max(s, axis=1, keepdims=True)
        p = jnp.exp(s - mx)
        den = jnp.sum(p, axis=1, keepdims=True)
        o = jnp.dot(p.astype(bf16), vv, preferred_element_type=f32) / den
        out = jnp.where(head, o, out)
        lse = jnp.where(head, mx + jnp.log(den), lse)
    o_ref[...] = out
    l_ref[...] = lse


def swa_prompt(aq, ak, av, group, *, nbatch):
    m = aq.shape[0]
    win, dil = SWA_GROUPS[group]
    n = SWA_N
    t = m // nbatch
    nb = t // (dil * n)
    rows = m // dil
    nblk = SWA_WIDTH // SWA_OUT
    view = lambda x: x.reshape(rows, dil * SWA_WIDTH)
    cur = pl.BlockSpec((n, SWA_OUT), lambda b, r, c: (b * nb + c, r * nblk + group))
    prev = pl.BlockSpec((n, SWA_OUT), lambda b, r, c: (b * nb + jnp.maximum(c - 1, 0), r * nblk + group))
    outs = pl.BlockSpec((n, SWA_OUT), lambda b, r, c: (b * nb + c, r))
    o, l = pl.pallas_call(
        _swa_kernel,
        grid=(nbatch, dil, nb),
        in_specs=[cur, cur, prev, cur, prev],
        out_specs=[outs, outs],
        out_shape=[jax.ShapeDtypeStruct((rows, dil * SWA_OUT), f32)] * 2,
        compiler_params=_params(("arbitrary",) * 3, 32),
        name=f"swa_prompt_w{win}",
    )(view(aq), view(ak), view(ak), view(av), view(av))
    return o.reshape(m, SWA_OUT), l.reshape(m, SWA_OUT)


def _merge_kernel(o0, o1, o2, l0, l1, l2, out_ref):
    ls = (l0[...], l1[...], l2[...])
    mx = jnp.maximum(jnp.maximum(ls[0], ls[1]), ls[2])
    es = [jnp.exp(l - mx) for l in ls]
    tot = es[0] + es[1] + es[2]
    out_ref[...] = (es[0] * o0[...] + es[1] * o1[...] + es[2] * o2[...]) / tot


def swa_merge(outs, lses):
    m = outs[0].shape[0]
    tm = 512
    spec = pl.BlockSpec((tm, SWA_OUT), lambda i: (i, 0))
    return pl.pallas_call(
        _merge_kernel, grid=(m // tm,), in_specs=[spec] * 6, out_specs=spec,
        out_shape=jax.ShapeDtypeStruct((m, SWA_OUT), f32),
        compiler_params=_params(("arbitrary",), 32), name="swa_merge",
    )(*outs, *lses)


def _swa_sample_kernel(nvalid, q_ref, k_ref, v_ref, b0_ref, b1_ref, b2_ref, bd_ref, o_ref):
    bd = bd_ref[...]
    n = SWA_N
    bufs = (b0_ref, b1_ref, b2_ref)
    arow = lax.broadcasted_iota(jnp.int32, (n, SWA_OUT), 0)
    nrow = lax.broadcasted_iota(jnp.int32, (SAMPLE_PAD, SWA_OUT), 0)
    rows = []
    for l in range(nvalid):
        outs, lses = [], []
        for g, (win, dil) in enumerate(SWA_GROUPS):
            cols = slice(g * SWA_OUT, (g + 1) * SWA_OUT)
            ql = q_ref[l:l + 1, cols]
            kn, vn = k_ref[:, cols], v_ref[:, cols]
            r = l % dil
            kb = bufs[g][:, r * 2 * SWA_OUT:r * 2 * SWA_OUT + SWA_OUT]
            vb = bufs[g][:, r * 2 * SWA_OUT + SWA_OUT:(r + 1) * 2 * SWA_OUT]
            scale = SWA_HD ** -0.5
            sb = jnp.dot(kb * ql, bd, preferred_element_type=f32, precision=lax.Precision.HIGHEST) * scale
            sn = jnp.dot(kn * ql, bd, preferred_element_type=f32, precision=lax.Precision.HIGHEST) * scale
            sb = jnp.where(arow * dil + r >= l, sb, -jnp.inf)
            ok_new = (nrow <= l) & (nrow < nvalid) & ((l - nrow) % dil == 0)
            sn = jnp.where(ok_new, sn, -jnp.inf)
            mx = jnp.maximum(jnp.max(sb, axis=0, keepdims=True), jnp.max(sn, axis=0, keepdims=True))
            pb, pn = jnp.exp(sb - mx), jnp.exp(sn - mx)
            den = jnp.sum(pb, axis=0, keepdims=True) + jnp.sum(pn, axis=0, keepdims=True)
            o = (jnp.sum(pb * vb, axis=0, keepdims=True) + jnp.sum(pn * vn, axis=0, keepdims=True)) / den
            outs.append(o)
            lses.append(mx + jnp.log(den))
        mx = jnp.maximum(jnp.maximum(lses[0], lses[1]), lses[2])
        es = [jnp.exp(x - mx) for x in lses]
        rows.append((es[0] * outs[0] + es[1] * outs[1] + es[2] * outs[2]) / (es[0] + es[1] + es[2]))
    rows.append(jnp.zeros((SAMPLE_PAD - nvalid, SWA_OUT), f32))
    o_ref[...] = jnp.concatenate(rows, axis=0)


def swa_sample(aq, ak, av, caches, *, nbatch, nvalid):
    m = aq.shape[0]
    bd = np.kron(np.eye(SWA_HEADS), np.ones((SWA_HD, SWA_HD))).astype(np.float32)
    tok = pl.BlockSpec((SAMPLE_PAD, SWA_WIDTH), lambda b: (b, 0))
    views, specs = [], []
    for (win, dil), buf in zip(SWA_GROUPS, caches):
        assert buf.shape[1] == win and nvalid <= dil * 4 and nvalid <= SWA_N
        views.append(buf.reshape(nbatch, win // dil, dil * 2 * SWA_OUT))
        width = min(dil, nvalid) * 2 * SWA_OUT
        specs.append(pl.BlockSpec((None, SWA_N, width), lambda b: (b, 0, 0)))
    return pl.pallas_call(
        functools.partial(_swa_sample_kernel, nvalid),
        grid=(nbatch,),
        in_specs=[tok, tok, tok, *specs, pl.BlockSpec((SWA_OUT, SWA_OUT), lambda b: (0, 0))],
        out_specs=pl.BlockSpec((SAMPLE_PAD, SWA_OUT), lambda b: (b, 0)),
        out_shape=jax.ShapeDtypeStruct((m, SWA_OUT), f32),
        compiler_params=_params(("arbitrary",), 32),
        name="swa_sample",
    )(aq, ak, av, *views, jnp.asarray(bd))


def _hgrn_kernel(layer, sub, valid, q_ref, f_ref, i_ref, g_ref, gam_ref, gain_ref, s0_ref, o_ref, s_ref, st_ref):
    blk = pl.program_id(1)
    nblk = pl.num_programs(1)
    length = q_ref.shape[0]

    @pl.when(blk == 0)
    def _():
        for h in range(HG_HEADS):
            st_ref[h] = s0_ref[h].T

    sm = jax.nn.softmax(gam_ref[...], axis=0)
    run = sm[0:1]
    for kk in range(1, layer + 1):
        run = run + sm[kk:kk + 1]
    lb = run - sm[0:1]
    log_lb, log_1m = jnp.log(lb), jnp.log1p(-lb)
    gain = gain_ref[...]
    r = lax.broadcasted_iota(jnp.int32, (sub, sub), 0)
    c = lax.broadcasted_iota(jnp.int32, (sub, sub), 1)
    tril = jnp.where(r >= c, 1.0, 0.0)
    bi = lax.broadcasted_iota(jnp.int32, (HG_HEADS * sub, HG_HEADS * sub * sub), 0)
    bf = lax.broadcasted_iota(jnp.int32, (HG_HEADS * sub, HG_HEADS * sub * sub), 1)
    bsel = jnp.where((bf // sub == bi) & (bf % sub <= bi % sub), 1.0, 0.0).astype(bf16)
    ones = jnp.ones((LANE, LANE), bf16)

    def step(sc, carry):
        r0 = pl.multiple_of(sc * sub, sub)
        rows = pl.ds(r0, sub)
        q, fr, v, gg = q_ref[rows, :], f_ref[rows, :], i_ref[rows, :], g_ref[rows, :]
        lg = jnp.logaddexp(log_lb, log_1m + jax.nn.log_sigmoid(fr))
        kin = (1.0 - lb) * jax.nn.sigmoid(-fr)
        if valid < length:
            tok = lax.broadcasted_iota(jnp.int32, (sub, 1), 0) + r0
            lg = jnp.where(tok < valid, lg, 0.0)
            kin = jnp.where(tok < valid, kin, 0.0)
        gc = jnp.dot(tril, lg, preferred_element_type=f32, precision=lax.Precision.HIGHEST)
        ge = gc[sub - 1:sub, :]
        qd = (q * jnp.exp(gc)).astype(bf16)
        kd = (kin * jnp.exp(ge - gc)).astype(bf16)
        dge = jnp.exp(ge)
        gate = gg * jax.nn.sigmoid(gg)
        prods, vts = [], []
        for h in range(HG_HEADS):
            ln = slice(h * LANE, (h + 1) * LANE)
            gch, kh, qh = gc[:, ln], kin[:, ln], q[:, ln]
            prods += [jnp.exp(jnp.minimum(gch[i:i + 1, :] - gch, 0.0)) * kh * qh[i:i + 1, :] for i in range(sub)]
            vts += [v[:, ln]] * sub
        red = jnp.dot(jnp.concatenate(prods, axis=0).astype(bf16), ones, preferred_element_type=f32)
        o_intra = jnp.dot(bsel, (red * jnp.concatenate(vts, axis=0)).astype(bf16), preferred_element_type=f32)
        for h in range(HG_HEADS):
            ln = slice(h * LANE, (h + 1) * LANE)
            st = st_ref[h]
            o = o_intra[h * sub:(h + 1) * sub, :] + lax.dot_general(qd[:, ln], st.astype(bf16), (((1,), (1,)), ((), ())),
                                                                    preferred_element_type=f32)
            st_ref[h] = st * dge[:, ln] + lax.dot_general(v[:, ln].astype(bf16), kd[:, ln], (((0,), (0,)), ((), ())),
                                                          preferred_element_type=f32)
            on = o * lax.rsqrt(jnp.mean(o * o, axis=1, keepdims=True) + EPS) * gain[:, ln]
            o_ref[rows, ln] = on * gate[:, ln]
        return carry

    lax.fori_loop(0, length // sub, step, 0)

    @pl.when(blk == nblk - 1)
    def _():
        for h in range(HG_HEADS):
            s_ref[h] = st_ref[h].T


def hgrn2(proj, lb_gamma, og_gain, s0, *, layer, nbatch, block, sub, valid):
    m = proj.shape[0]
    nblk = (m // nbatch) // block
    depth = lb_gamma.shape[0]
    tok = lambda off: pl.BlockSpec((block, D_MODEL), lambda b, c: (b * nblk + c, off))
    st = pl.BlockSpec((None, HG_HEADS, LANE, LANE), lambda b, c: (b, 0, 0, 0))
    return pl.pallas_call(
        functools.partial(_hgrn_kernel, layer, sub, valid),
        grid=(nbatch, nblk),
        in_specs=[tok(0), tok(1), tok(2), tok(3),
                  pl.BlockSpec((depth, D_MODEL), lambda b, c: (0, 0)),
                  pl.BlockSpec((1, D_MODEL), lambda b, c: (0, 0)), st],
        out_specs=[pl.BlockSpec((block, D_MODEL), lambda b, c: (b * nblk + c, 0)), st],
        out_shape=[jax.ShapeDtypeStruct((m, D_MODEL), f32), jax.ShapeDtypeStruct(s0.shape, f32)],
        scratch_shapes=[pltpu.VMEM((HG_HEADS, LANE, LANE), f32)],
        compiler_params=_params(("arbitrary",) * 2, 40),
        name="hgrn2",
    )(proj, proj, proj, proj, lb_gamma, og_gain.reshape(1, D_MODEL), s0)


def _cand_pairs():
    return [(i, j) for i in range(PEER_TOPK) for j in range(PEER_TOPK) if (i + 1) * (j + 1) <= PEER_TOPK]


def _top_rows(x, count, with_rank=False):
    rows, cur = [], x
    rank = jnp.full(x.shape, float(count), f32) if with_rank else None
    for r in range(count):
        mx = jnp.max(cur, axis=0, keepdims=True)
        rows.append(mx)
        hit = cur == mx
        if with_rank:
            rank = jnp.where(hit, float(r), rank)
        cur = jnp.where(hit, -jnp.inf, cur)
    return rows, rank


def _route_kernel(q_ref, keys_ref, rk_ref, ki_ref, wa_ref, eb_ref):
    pairs = _cand_pairs()
    pad = (-len(pairs)) % 8
    for h in range(PEER_HEADS):
        qh = q_ref[:, h * LANE:(h + 1) * LANE].astype(bf16)
        nt = (((1,), (1,)), ((), ()))
        a = lax.dot_general(keys_ref[2 * h], qh, nt, preferred_element_type=f32)
        b = lax.dot_general(keys_ref[2 * h + 1], qh, nt, preferred_element_type=f32)
        ta, _ = _top_rows(a, PEER_TOPK)
        tb, rank_b = _top_rows(b, PEER_TOPK, with_rank=True)
        cand = jnp.concatenate([ta[i] + tb[j] for i, j in pairs]
                               + [jnp.full((pad, a.shape[1]), -jnp.inf, f32)], axis=0)
        tau = _top_rows(cand, PEER_TOPK)[0][-1]
        top = ta[0] + tb[0]
        z = jnp.sum(jnp.where(cand >= tau, jnp.exp(cand - top), 0.0), axis=0, keepdims=True)
        thr = tau - a
        count = jnp.zeros_like(a)
        for r in range(PEER_TOPK):
            count = count + jnp.where(tb[r] >= thr, 1.0, 0.0)
        rows = slice(h * PEER_KEYS, (h + 1) * PEER_KEYS)
        rk_ref[rows, :] = rank_b.astype(bf16)
        ki_ref[rows, :] = count
        wa_ref[rows, :] = jnp.exp(a - ta[0]) / z
        eb_ref[rows, :] = jnp.exp(b - tb[0]).astype(bf16)


def peer_route(q, keys_pad):
    m = q.shape[0]
    tm = TOK_CHUNK
    per = _peer_chunk(m) // tm
    out = pl.BlockSpec((None, PEER_HEADS * PEER_KEYS, tm), lambda i: (i // per, 0, i % per))
    shape = lambda dt: jax.ShapeDtypeStruct((m // (per * tm), PEER_HEADS * PEER_KEYS, per * tm), dt)
    return pl.pallas_call(
        _route_kernel,
        grid=(m // tm,),
        in_specs=[pl.BlockSpec((tm, D_MODEL), lambda i: (i, 0)),
                  pl.BlockSpec(keys_pad.shape, lambda i: (0, 0, 0))],
        out_specs=[out] * 4,
        out_shape=[shape(bf16), shape(f32), shape(f32), shape(bf16)],
        compiler_params=_params(("arbitrary",), 40),
        name="peer_route",
    )(q, keys_pad)


def _gelu(x):
    return 0.5 * x * (1.0 + lax.erf(x * (2.0 ** -0.5)))


def _table_prep_kernel(u_ref, v_ref, uo_ref, vo_ref):
    uo_ref[...] = u_ref[...].astype(bf16)
    vo_ref[...] = v_ref[...].astype(bf16).T


def peer_tables(u, v):
    depth, e, d = u.shape
    te = 512
    return pl.pallas_call(
        _table_prep_kernel,
        grid=(depth, e // te),
        in_specs=[pl.BlockSpec((None, te, d), lambda l, j: (l, j, 0))] * 2,
        out_specs=[pl.BlockSpec((None, te, d), lambda l, j: (l, j, 0)),
                   pl.BlockSpec((None, d, te), lambda l, j: (l, 0, j))],
        out_shape=[jax.ShapeDtypeStruct((depth, e, d), bf16), jax.ShapeDtypeStruct((depth, d, e), bf16)],
        compiler_params=_params(("arbitrary", "arbitrary"), 32),
        name="peer_tables",
    )(u, v)


def _peer_dense_kernel(ht_ref, u_ref, vt_ref, rk_ref, ki_ref, wa_ref, eb_ref, x_ref, g_ref, o_ref,
                       acc_ref, act_ref, coef_ref):
    j = pl.program_id(1)
    nj = pl.num_programs(1)
    te, tm = act_ref.shape
    ni = te // PEER_KEYS
    width = 8 * 16 * LANE // tm

    @pl.when(j == 0)
    def _():
        acc_ref[...] = jnp.zeros_like(acc_ref)

    act_ref[...] = _gelu(jnp.dot(u_ref[...], ht_ref[...], preferred_element_type=f32)).astype(bf16)
    for ii in range(ni):
        i = j * ni + ii
        counts = [ki_ref[pl.ds(h * PEER_KEYS + i, 1), :].astype(bf16) for h in range(PEER_HEADS)]
        was = [wa_ref[pl.ds(h * PEER_KEYS + i, 1), :].astype(bf16) for h in range(PEER_HEADS)]
        for part in range(PEER_KEYS // width):
            gate = jnp.zeros((width, tm), bf16)
            for h in range(PEER_HEADS):
                rows = slice(h * PEER_KEYS + part * width, h * PEER_KEYS + (part + 1) * width)
                gate = gate + jnp.where(rk_ref[rows, :] < counts[h], eb_ref[rows, :], 0) * was[h]
            blk = slice(ii * PEER_KEYS + part * width, ii * PEER_KEYS + (part + 1) * width)
            coef_ref[blk, :] = gate * act_ref[blk, :]
    acc_ref[...] += jnp.dot(vt_ref[...], coef_ref[...], preferred_element_type=f32)

    @pl.when(j == nj - 1)
    def _():
        o_ref[...] = x_ref[...] + g_ref[...] * acc_ref[...].T


def peer_dense(ht, u, vt, layer, route, x, gate):
    m, d = x.shape
    tm = _peer_chunk(m)
    te = 2048
    rt = pl.BlockSpec((None, PEER_HEADS * PEER_KEYS, tm), lambda i, j: (i, 0, 0))
    assert ht.shape == (m // tm, d, tm)
    return pl.pallas_call(
        _peer_dense_kernel,
        grid=(m // tm, PEER_EXPERTS // te),
        in_specs=[pl.BlockSpec((None, d, tm), lambda i, j: (i, 0, 0)),
                  pl.BlockSpec((None, te, d), lambda i, j: (layer, j, 0)),
                  pl.BlockSpec((None, d, te), lambda i, j: (layer, 0, j)),
                  rt, rt, rt, rt,
                  pl.BlockSpec((tm, d), lambda i, j: (i, 0)),
                  _mod_spec(gate, m, tm, 2)],
        out_specs=pl.BlockSpec((tm, d), lambda i, j: (i, 0)),
        out_shape=jax.ShapeDtypeStruct((m, d), f32),
        scratch_shapes=[pltpu.VMEM((d, tm), f32), pltpu.VMEM((te, tm), bf16), pltpu.VMEM((te, tm), bf16)],
        compiler_params=_params(("arbitrary", "arbitrary"), 56),
        name="peer_dense",
    )(ht, u, vt, *route, x, gate)


def peer_layer(x, g, sc, sh, gate, w_q, keys_pad, u, vt, layer):
    q, ht = norm_mod_matmul(x, g, sc, sh, w_q, emit_ht=True)
    route = peer_route(q, keys_pad)
    return peer_dense(ht, u, vt, layer, route, x, gate)


def _prep_weights(p):
    w = {}
    ev = p['even_w_in'][0]
    cuts = np.cumsum([ML_WIDTH] * 4 + [2 * ML_HEADS] + [SWA_WIDTH] * 2)
    mq_mo, mg, att = ev[:, :cuts[3]], ev[:, cuts[3]:cuts[4]], ev[:, cuts[4]:]
    pad = jnp.zeros((D_MODEL, 2 * LANE - 2 * ML_HEADS), ev.dtype)
    w['even_in'] = jnp.concatenate([att, mg, pad, mq_mo], axis=1).astype(bf16)
    wo = p['even_w_out'][0].astype(bf16)
    w['even_out'] = (wo[:ML_WIDTH], wo[ML_WIDTH:])
    w['odd_in'] = p['odd_w_in'][0].astype(bf16)
    w['odd_out'] = p['odd_w_out'][0].astype(bf16)
    w['peer_q'] = p['peer_w_q'].astype(bf16)
    sk = p['peer_sub_keys'].astype(bf16)
    zero = jnp.zeros_like(sk)
    half0 = jnp.concatenate([sk[:, :, 0], zero[:, :, 0]], axis=-1)
    half1 = jnp.concatenate([zero[:, :, 1], sk[:, :, 1]], axis=-1)
    depth = sk.shape[0]
    w['peer_keys'] = jnp.stack([half0, half1], axis=2).reshape(depth, 2 * PEER_HEADS, PEER_KEYS, LANE)
    w['peer_u'], w['peer_vt'] = peer_tables(p['peer_u'], p['peer_v'])
    return w


def _trunk(x, mods, tables, p, w, st, *, nbatch, chunk, valid, hg_block, hg_sub):
    m = x.shape[0]
    res = {}
    sh1, sc1, g1, sh2, sc2, g2 = mods[0]
    proj = norm_mod_matmul(x, p['norm_g'][0, 0], sc1, sh1, w['even_in'])
    if st is None:
        c0 = jnp.zeros((nbatch, ML_HEADS, ML_DK, LANE), f32)
        n0 = jnp.zeros((nbatch, ML_HEADS, 1, LANE), f32)
        m0 = jnp.zeros((nbatch, ML_HEADS, 1, LANE), f32)
    else:
        c0 = st['C'][0].astype(f32)
        n0 = st['n'][0].astype(f32)[:, :, None, :]
        m0 = jnp.broadcast_to(st['m'][0].astype(f32)[:, :, None, None], (nbatch, ML_HEADS, 1, LANE))
    hm, c_new, n_new, m_new = mlstm(proj, p['even_b_if'][0], p['even_ml_gain'][0], c0, n0, m0,
                                    nbatch=nbatch, chunk=chunk, valid=valid)
    res['C'], res['n'], res['m'] = c_new, n_new[:, :, 0, :], m_new[:, :, 0, 0]
    aq, ak, av = qk_prep(proj, p['even_qn_g'][0], p['even_kn_g'][0], tables)
    if st is None:
        outs, lses = zip(*[swa_prompt(aq, ak, av, g, nbatch=nbatch) for g in range(len(SWA_GROUPS))])
        ha = swa_merge(outs, lses)
    else:
        ha = swa_sample(aq, ak, av, [b[0] for b in st['swa']], nbatch=nbatch, nvalid=valid)
    res['kv'] = (ak, av)
    x = resid_matmul(x, g1, [hm, ha], list(w['even_out']))
    x = peer_layer(x, p['norm_g'][0, 1], sc2, sh2, g2, w['peer_q'][0], w['peer_keys'][0], w['peer_u'], w['peer_vt'],0)
    sh1, sc1, g1, sh2, sc2, g2 = mods[1]
    proj = norm_mod_matmul(x, p['norm_g'][1, 0], sc1, sh1, w['odd_in'])
    s0 = jnp.zeros((nbatch, HG_HEADS, LANE, LANE), f32) if st is None else st['S'][0].astype(f32)
    ho, s_new = hgrn2(proj, p['odd_lb_gamma'], p['odd_og_gain'][0], s0, layer=1, nbatch=nbatch,
                      block=hg_block, sub=hg_sub, valid=valid)
    res['S'] = s_new
    x = resid_matmul(x, g1, [ho], [w['odd_out']])
    x = peer_layer(x, p['norm_g'][1, 1], sc2, sh2, g2, w['peer_q'][1], w['peer_keys'][1], w['peer_u'], w['peer_vt'],1)
    res['y'] = x
    return res


def kernel(x_prompt, x_sample, c_prompt, c_sample, state_mlstm_C, state_mlstm_n, state_mlstm_m, cache_swa_w128, cache_swa_w512, cache_swa_w2048, state_hgrn_S, norm_g, ada_w, ada_b, even_w_in, even_b_if, even_ml_gain, even_qn_g, even_kn_g, even_w_out, odd_w_in, odd_lb_gamma, odd_og_gain, odd_w_out, peer_w_q, peer_sub_keys, peer_u, peer_v):
    p = {'norm_g': norm_g, 'even_w_in': even_w_in, 'even_b_if': even_b_if, 'even_ml_gain': even_ml_gain,
         'even_qn_g': even_qn_g, 'even_kn_g': even_kn_g, 'even_w_out': even_w_out, 'odd_w_in': odd_w_in,
         'odd_lb_gamma': odd_lb_gamma, 'odd_og_gain': odd_og_gain, 'odd_w_out': odd_w_out, 'peer_w_q': peer_w_q,
         'peer_sub_keys': peer_sub_keys, 'peer_u': peer_u, 'peer_v': peer_v}
    w = _prep_weights(p)
    bp, tp, d = x_prompt.shape
    bs, ts, _ = x_sample.shape
    depth = ada_w.shape[0]
    assert tp % (SWA_GROUPS[-1][0]) == 0 and ts <= SAMPLE_PAD // 2 and (bs * SAMPLE_PAD) % TOK_CHUNK == 0

    nrows = -(-(bp + bs) // 8) * 8
    c_all = jnp.zeros((nrows, d), f32).at[:bp].set(c_prompt).at[bp:bp + bs].set(c_sample)
    mod = ada_all(c_all, ada_w, ada_b)
    split = lambda a: [a[..., k * d:(k + 1) * d] for k in range(6)]
    mods_p = [[u[:, None, :] for u in split(mod[l, :bp])] for l in range(depth)]
    mods_s = [[jnp.repeat(u, SAMPLE_PAD, axis=0)[None] for u in split(mod[l, bp:bp + bs])] for l in range(depth)]

    tables_p = _rotary_tables(jnp.arange(tp))
    rp = _trunk(x_prompt.reshape(bp * tp, d), mods_p, tables_p, p, w, None,
                nbatch=bp, chunk=ML_CHUNK, valid=ML_CHUNK, hg_block=ML_CHUNK, hg_sub=HG_SUB)

    xs = jnp.zeros((bs, SAMPLE_PAD, d), f32).at[:, :ts].set(x_sample).reshape(bs * SAMPLE_PAD, d)
    pos_s = jnp.tile(PAST_LEN + jnp.arange(SAMPLE_PAD), bs)
    tables_s = _rotary_tables(pos_s)
    st = {'C': state_mlstm_C, 'n': state_mlstm_n, 'm': state_mlstm_m,
          'swa': (cache_swa_w128, cache_swa_w512, cache_swa_w2048), 'S': state_hgrn_S}
    rs = _trunk(xs, mods_s, tables_s, p, w, st,
                nbatch=bs, chunk=SAMPLE_PAD, valid=ts, hg_block=SAMPLE_PAD, hg_sub=SAMPLE_PAD)

    def kv_rows(r, nb, t, keep_from, keep_to):
        ak, av = r['kv']
        out = []
        for g in range(len(SWA_GROUPS)):
            cols = slice(g * SWA_OUT, (g + 1) * SWA_OUT)
            k = ak.reshape(nb, t, SWA_WIDTH)[:, :, cols].reshape(nb, t, SWA_HEADS, SWA_HD)
            v = av.reshape(nb, t, SWA_WIDTH)[:, :, cols].reshape(nb, t, SWA_HEADS, SWA_HD)
            lo = keep_from(g)
            out.append(jnp.stack([k, v], axis=2)[:, lo:keep_to][None])
        return out

    rows_p = kv_rows(rp, bp, tp, lambda g: tp - min(SWA_GROUPS[g][0], tp), tp)
    rows_s = kv_rows(rs, bs, SAMPLE_PAD, lambda g: 0, ts)
    y_p = rp['y'].reshape(bp, tp, d)
    y_s = rs['y'].reshape(bs, SAMPLE_PAD, d)[:, :ts]
    return (y_p, y_s, rp['C'][None], rs['C'][None], rp['n'][None], rs['n'][None], rp['m'][None], rs['m'][None],
            rows_p[0], rows_s[0], rows_p[1], rows_s[1], rows_p[2], rows_s[2], rp['S'][None], rs['S'][None])
```
